```python
import jax, jax.numpy as jnp
from jax import lax
import numpy as np

D_MODEL = 4096
BATCH = 2
SEQ = 4096
DEPTH = 4
DEC_BATCH = 16
DEC_SEQ = 32
PAST_LEN = 1024

CHUNK = 64
BAND_CHUNKS = 8
BAND = BAND_CHUNKS * CHUNK
HEAD_DIM = 128
A_WIDTH = 3 * D_MODEL // 8
A_HEADS = A_WIDTH // HEAD_DIM
REL_CLIP = 128
B_WIDTH = 3 * D_MODEL // 8
B_HEADS = B_WIDTH // HEAD_DIM
C_WIDTH = D_MODEL // 4
C_GROUPS = 4
C_GROUP_WIDTH = C_WIDTH // C_GROUPS
POOL_WINDOWS = (2, 4, 8, 16)
POOL_CTX = 15
MIX_WIDTH = A_WIDTH + B_WIDTH + C_WIDTH
IN_WIDTH = 3 * A_WIDTH + 4 * B_WIDTH + C_WIDTH
SPLIT_POINTS = (A_WIDTH, 2 * A_WIDTH, 3 * A_WIDTH, 3 * A_WIDTH + B_WIDTH,
                3 * A_WIDTH + 2 * B_WIDTH, 3 * A_WIDTH + 3 * B_WIDTH, 3 * A_WIDTH + 4 * B_WIDTH)
D_FF = 4 * D_MODEL
ROPE_BASE = 10000.0
EPS = 1e-6
NEG_INF = -1e30
F32 = jnp.float32

kernel_name = 'hybrid_streaming_encoder_step'


def rmsnorm(x, g):
    xf = x.astype(F32)
    y = xf * lax.rsqrt(jnp.mean(xf * xf, axis=-1, keepdims=True) + EPS) * g.astype(F32)
    return y.astype(x.dtype)


def rotary(x, pos):
    half = HEAD_DIM // 2
    inv = ROPE_BASE ** (-jnp.arange(half, dtype=F32) / half)
    ang = pos[:, None] * inv[None, :]
    cos = jnp.cos(ang)[None, :, None, :]
    sin = jnp.sin(ang)[None, :, None, :]
    xf = x.astype(F32)
    x1, x2 = xf[..., :half], xf[..., half:]
    return jnp.concatenate([x1 * cos - x2 * sin, x1 * sin + x2 * cos], axis=-1).astype(x.dtype)


def split_projection(xn, w):
    z = jnp.einsum('bld,de->ble', xn, w)
    b, L, _ = z.shape
    aq, ak, av, bq, bk, bv, bg, cu = jnp.split(z, SPLIT_POINTS, axis=-1)
    ha = lambda t: t.reshape(b, L, A_HEADS, HEAD_DIM)
    hb = lambda t: t.reshape(b, L, B_HEADS, HEAD_DIM)
    return ha(aq), ha(ak), ha(av), hb(bq), hb(bk), hb(bv), bg, cu


def rel_bias(table, rel):
    idx = jnp.clip(rel, -REL_CLIP, REL_CLIP) + REL_CLIP
    return jnp.take(table, idx, axis=1).astype(F32)


def band_attention_prompt(q, k, v, table):
    b, L, H, dh = q.shape
    nc = L // CHUNK
    span = BAND + CHUNK
    qc = q.reshape(b, nc, CHUNK, H, dh)
    pad = jnp.zeros((b, BAND, H, dh), k.dtype)
    kc = jnp.concatenate([pad, k], axis=1).reshape(b, nc + BAND_CHUNKS, CHUNK, H, dh)
    vc = jnp.concatenate([pad.astype(v.dtype), v], axis=1).reshape(b, nc + BAND_CHUNKS, CHUNK, H, dh)
    idx = jnp.arange(nc)[:, None] + jnp.arange(BAND_CHUNKS + 1)[None, :]
    kb = kc[:, idx].reshape(b, nc, span, H, dh)
    vb = vc[:, idx].reshape(b, nc, span, H, dh)
    rel = jnp.arange(CHUNK)[:, None] - jnp.arange(span)[None, :] + BAND
    bias = rel_bias(table, rel)
    kpos = (jnp.arange(nc)[:, None] - BAND_CHUNKS) * CHUNK + jnp.arange(span)[None, :]
    valid = kpos >= 0
    s = jnp.einsum('bnqhd,bnkhd->bnhqk', qc, kb).astype(F32) * (dh ** -0.5) + bias[None, None]
    s = jnp.where(valid[None, :, None, None, :], s, NEG_INF)
    p = jax.nn.softmax(s, axis=-1).astype(v.dtype)
    o = jnp.einsum('bnhqk,bnkhd->bnqhd', p, vb)
    return o.reshape(b, L, H * dh)


def band_attention_sample(q, k_all, v_all, table):
    b, S, H, dh = q.shape
    la = k_all.shape[1] - S
    rel = jnp.arange(S)[:, None] - jnp.arange(la + S)[None, :] + la
    bias = rel_bias(table, rel)
    s = jnp.einsum('bqhd,bkhd->bhqk', q, k_all).astype(F32) * (dh ** -0.5) + bias[None]
    p = jax.nn.softmax(s, axis=-1).astype(v_all.dtype)
    o = jnp.einsum('bhqk,bkhd->bqhd', p, v_all)
    return o.reshape(b, S, H * dh)


def retention_scan(q, k, v, s0, n_blocks):
    b, L, H, d = q.shape
    cb = L // n_blocks
    log_g = jnp.log1p(-jnp.exp2(-5.0 - jnp.arange(H, dtype=F32)))
    qb = q.astype(F32).reshape(b, n_blocks, cb, H, d)
    kb = k.astype(F32).reshape(b, n_blocks, cb, H, d)
    vb = v.astype(F32).reshape(b, n_blocks, cb, H, d)
    t = jnp.arange(cb, dtype=F32)
    diff = t[:, None] - t[None, :]
    decay = jnp.where(diff[None] >= 0, jnp.exp(jnp.maximum(diff, 0.0)[None] * log_g[:, None, None]), 0.0)
    scores = jnp.einsum('bnqhd,bnkhd->bnhqk', qb, kb) * decay
    o_inner = jnp.einsum('bnhqk,bnkhe->bnqhe', scores, vb)
    k_w = jnp.exp((cb - 1.0 - t)[:, None] * log_g[None, :])
    kv = jnp.einsum('bnkhd,bnkhe,kh->bnhde', kb, vb, k_w)
    g_block = jnp.exp(cb * log_g)[None, :, None, None]

    def step(s, kv_n):
        return s * g_block + kv_n, s

    s_fin, s_start = lax.scan(step, s0.astype(F32), jnp.moveaxis(kv, 1, 0))
    s_start = jnp.moveaxis(s_start, 0, 1)
    q_w = jnp.exp((t + 1.0)[:, None] * log_g[None, :])
    o_cross = jnp.einsum('bnqhd,bnhde,qh->bnqhe', qb, s_start, q_w)
    return (o_inner + o_cross).reshape(b, L, H, d), s_fin


def retention_branch(q, k, v, gate, s0, pos, gn_gain, n_blocks):
    q = rotary(q, pos)
    k = rotary(k, pos) * (HEAD_DIM ** -0.5)
    o, s_fin = retention_scan(q, k, v, s0, n_blocks)
    mu = jnp.mean(o, axis=-1, keepdims=True)
    var = jnp.mean(jnp.square(o - mu), axis=-1, keepdims=True)
    on = ((o - mu) * lax.rsqrt(var + EPS)).reshape(o.shape[0], o.shape[1], B_WIDTH) * gn_gain.astype(F32)
    y = jax.nn.silu(gate.astype(F32)) * on
    return y.astype(gate.dtype), s_fin


def pool_branch(ctx, u, pos, w_pool, scale):
    b, L, c = u.shape
    uf = jnp.concatenate([ctx.astype(F32), u.astype(F32)], axis=1)
    cs = jnp.concatenate([jnp.zeros((b, 1, c), F32), jnp.cumsum(uf, axis=1)], axis=1)
    pooled = []
    for g, w in enumerate(POOL_WINDOWS):
        sl = slice(g * C_GROUP_WIDTH, (g + 1) * C_GROUP_WIDTH)
        win = cs[:, POOL_CTX + 1:POOL_CTX + 1 + L, sl] - cs[:, POOL_CTX + 1 - w:POOL_CTX + 1 - w + L, sl]
        cnt = jnp.minimum(float(w), pos + 1.0)[None, :, None]
        pooled.append(win / cnt)
    diff = (jnp.concatenate(pooled, axis=-1) - u.astype(F32)).reshape(b, L, C_GROUPS, C_GROUP_WIDTH)
    y = jnp.einsum('blgc,gce->blge', diff, w_pool.astype(F32)).reshape(b, L, c) * scale.astype(F32)
    return y.astype(u.dtype)


def finish_layer(h, mixed, w_out, g_ffn, w_up, w_down):
    h = h + jnp.einsum('blm,md->bld', mixed, w_out)
    a = jnp.einsum('bld,df->blf', rmsnorm(h, g_ffn), w_up)
    return h + jnp.einsum('blf,fd->bld', jnp.square(jax.nn.relu(a)), w_down)


def setup_inputs(seed: int = 0) -> dict:
    key = jax.random.key(seed)
    ks = jax.random.split(key, 18)
    la = min(BAND, PAST_LEN)
    nrm = lambda k, shape, s: jax.random.normal(k, shape, F32) * s
    return {
        'x_prompt': nrm(ks[0], (BATCH, SEQ, D_MODEL), 1.0),
        'x_sample': nrm(ks[1], (DEC_BATCH, DEC_SEQ, D_MODEL), 1.0),
        'cache_band_k': nrm(ks[2], (DEPTH, DEC_BATCH, la, A_HEADS, HEAD_DIM), 1.0),
        'cache_band_v': nrm(ks[3], (DEPTH, DEC_BATCH, la, A_HEADS, HEAD_DIM), 1.0),
        'state_retention': nrm(ks[4], (DEPTH, DEC_BATCH, B_HEADS, HEAD_DIM, HEAD_DIM), 0.5),
        'state_pool': nrm(ks[5], (DEPTH, DEC_BATCH, POOL_CTX, C_WIDTH), 1.0),
        'norm_mix': 1.0 + nrm(ks[6], (DEPTH, D_MODEL), 0.01),
        'w_in': nrm(ks[7], (DEPTH, D_MODEL, IN_WIDTH), D_MODEL ** -0.5),
        'rel_bias_table': nrm(ks[8], (DEPTH, A_HEADS, 2 * REL_CLIP + 1), 0.1),
        'ret_norm': 1.0 + nrm(ks[9], (DEPTH, B_WIDTH), 0.01),
        'pool_w': nrm(ks[10], (DEPTH, C_GROUPS, C_GROUP_WIDTH, C_GROUP_WIDTH), C_GROUP_WIDTH ** -0.5),
        'pool_scale': 1.0 + nrm(ks[11], (DEPTH, C_WIDTH), 0.02),
        'w_out': nrm(ks[12], (DEPTH, MIX_WIDTH, D_MODEL), MIX_WIDTH ** -0.5),
        'norm_ffn': 1.0 + nrm(ks[13], (DEPTH, D_MODEL), 0.01),
        'w_up': nrm(ks[14], (DEPTH, D_MODEL, D_FF), D_MODEL ** -0.5),
        'w_down': nrm(ks[15], (DEPTH, D_FF, D_MODEL), D_FF ** -0.5),
        'norm_final': 1.0 + nrm(ks[16], (D_MODEL,), 0.01),
    }


def reference(x_prompt, x_sample, cache_band_k, cache_band_v, state_retention, state_pool,
              norm_mix, w_in, rel_bias_table, ret_norm, pool_w, pool_scale, w_out,
              norm_ffn, w_up, w_down, norm_final):
    bp, lp, _ = x_prompt.shape
    bs, ls, _ = x_sample.shape
    pos_p = jnp.arange(lp, dtype=F32)
    pos_s = PAST_LEN + jnp.arange(ls, dtype=F32)
    keep = min(BAND, lp)
    hp, hs = x_prompt, x_sample
    kp_l, vp_l, rp_l, pp_l = [], [], [], []
    ks_l, vs_l, rs_l, ps_l = [], [], [], []
    for l in range(DEPTH):
        aq, ak, av, bq, bk, bv, bg, cu = split_projection(rmsnorm(hp, norm_mix[l]), w_in[l])
        a_o = band_attention_prompt(aq, ak, av, rel_bias_table[l])
        s0 = jnp.zeros((bp, B_HEADS, HEAD_DIM, HEAD_DIM), F32)
        b_o, s_fin = retention_branch(bq, bk, bv, bg, s0, pos_p, ret_norm[l], lp // CHUNK)
        c_o = pool_branch(jnp.zeros((bp, POOL_CTX, C_WIDTH), cu.dtype), cu, pos_p, pool_w[l], pool_scale[l])
        hp = finish_layer(hp, jnp.concatenate([a_o, b_o, c_o], axis=-1), w_out[l], norm_ffn[l], w_up[l], w_down[l])
        kp_l.append(ak[:, lp - keep:])
        vp_l.append(av[:, lp - keep:])
        rp_l.append(s_fin)
        pp_l.append(cu[:, lp - POOL_CTX:])
        aq, ak, av, bq, bk, bv, bg, cu = split_projection(rmsnorm(hs, norm_mix[l]), w_in[l])
        k_all = jnp.concatenate([cache_band_k[l].astype(ak.dtype), ak], axis=1)
        v_all = jnp.concatenate([cache_band_v[l].astype(av.dtype), av], axis=1)
        a_o = band_attention_sample(aq, k_all, v_all, rel_bias_table[l])
        b_o, s_new = retention_branch(bq, bk, bv, bg, state_retention[l], pos_s, ret_norm[l], 1)
        ctx = state_pool[l].astype(cu.dtype)
        c_o = pool_branch(ctx, cu, pos_s, pool_w[l], pool_scale[l])
        hs = finish_layer(hs, jnp.concatenate([a_o, b_o, c_o], axis=-1), w_out[l], norm_ffn[l], w_up[l], w_down[l])
        ks_l.append(ak)
        vs_l.append(av)
        rs_l.append(s_new)
        ps_l.append(jnp.concatenate([ctx, cu], axis=1)[:, -POOL_CTX:])
    y_prompt = rmsnorm(hp, norm_final)
    y_sample = rmsnorm(hs, norm_final)
    return (y_prompt, y_sample,
            jnp.stack(kp_l), jnp.stack(vp_l), jnp.stack(rp_l), jnp.stack(pp_l),
            jnp.stack(ks_l), jnp.stack(vs_l), jnp.stack(rs_l), jnp.stack(ps_l))
```

```python
import functools

import jax
import jax.numpy as jnp
from jax import lax
from jax.experimental import pallas as pl
from jax.experimental.pallas import tpu as pltpu

F32 = jnp.float32
BF16 = jnp.bfloat16

PAST_LEN = 1024
CHUNK = 64
BAND_CHUNKS = 8
BAND = BAND_CHUNKS * CHUNK
HEAD_DIM = 128
REL_CLIP = 128
TABLE = 2 * REL_CLIP + 1
POOL_WINDOWS = (2, 4, 8, 16)
POOL_HALO = 16
ROPE_BASE = 10000.0
EPS = 1e-6
NEG_INF = -1e30

LANES = 128
SUBLANES_BF16 = 16
VMEM_LIMIT_BYTES = 56 * 1024 * 1024

ATTN_TQ = 256
RET_BLOCK = 256
NORM_ROWS = 64


def _pick_tile(n, target, align):
    best = None
    for t in range(align, min(n, target) + 1, align):
        if n % t == 0:
            best = t
    if best is None:
        raise ValueError(f"no tile for {n} (align {align}, target {target})")
    return best


def _params(*sem):
    return pltpu.CompilerParams(dimension_semantics=sem, vmem_limit_bytes=VMEM_LIMIT_BYTES)


def _norm_rows(x_ref, g_ref, xn_ref):
    def body(c, carry):
        r = pl.multiple_of(c * NORM_ROWS, NORM_ROWS)
        x = x_ref[pl.ds(r, NORM_ROWS), :]
        ms = jnp.mean(x * x, axis=-1, keepdims=True)
        xn_ref[pl.ds(r, NORM_ROWS), :] = (x * lax.rsqrt(ms + EPS) * g_ref[...]).astype(BF16)
        return carry
    lax.fori_loop(0, x_ref.shape[0] // NORM_ROWS, body, 0)


def _norm_matmul_kernel(x_ref, g_ref, w_ref, o_ref, xn_ref, *, relu2):
    @pl.when(pl.program_id(1) == 0)
    def _():
        _norm_rows(x_ref, g_ref, xn_ref)
    y = jnp.dot(xn_ref[...], w_ref[...], preferred_element_type=F32)
    if relu2:
        y = jnp.square(jnp.maximum(y, 0.0))
    o_ref[...] = y.astype(o_ref.dtype)


def _norm_matmul(x, g, w, *, relu2, out_dtype, tn_target):
    t, k = x.shape
    n = w.shape[1]
    tm = _pick_tile(t, 1100, NORM_ROWS)
    tn = _pick_tile(n, tn_target, LANES)
    return pl.pallas_call(
        functools.partial(_norm_matmul_kernel, relu2=relu2),
        grid=(t // tm, n // tn),
        in_specs=[
            pl.BlockSpec((tm, k), lambda i, j: (i, 0), pipeline_mode=pl.Buffered(1)),
            pl.BlockSpec((1, k), lambda i, j: (0, 0)),
            pl.BlockSpec((k, tn), lambda i, j: (0, j)),
        ],
        out_specs=pl.BlockSpec((tm, tn), lambda i, j: (i, j)),
        out_shape=jax.ShapeDtypeStruct((t, n), out_dtype),
        scratch_shapes=[pltpu.VMEM((tm, k), BF16)],
        compiler_params=_params("parallel", "arbitrary"),
        name="norm_matmul_relu2" if relu2 else "norm_matmul",
    )(x, g.reshape(1, k), w)


def _outproj_kernel(a_ref, b_ref, c_ref, wa_ref, wb_ref, wc_ref, h_ref, o_ref):
    acc = jnp.dot(a_ref[...], wa_ref[...], preferred_element_type=F32)
    acc += jnp.dot(b_ref[...], wb_ref[...], preferred_element_type=F32)
    acc += jnp.dot(c_ref[...], wc_ref[...], preferred_element_type=F32)
    o_ref[...] = h_ref[...] + acc


def _outproj(h, a_o, b_o, c_o, w_out):
    t, d = h.shape
    wa, wb, wc = a_o.shape[1], b_o.shape[1], c_o.shape[1]
    assert wa == wb and (wa + wb) % wc == 0
    tm = _pick_tile(t, 1100, SUBLANES_BF16)
    tn = _pick_tile(d, 512, LANES)
    return pl.pallas_call(
        _outproj_kernel,
        grid=(t // tm, d // tn),
        in_specs=[
            pl.BlockSpec((tm, wa), lambda i, j: (i, 0)),
            pl.BlockSpec((tm, wb), lambda i, j: (i, 0)),
            pl.BlockSpec((tm, wc), lambda i, j: (i, 0)),
            pl.BlockSpec((wa, tn), lambda i, j: (0, j)),
            pl.BlockSpec((wb, tn), lambda i, j: (1, j)),
            pl.BlockSpec((wc, tn), lambda i, j: ((wa + wb) // wc, j)),
            pl.BlockSpec((tm, tn), lambda i, j: (i, j)),
        ],
        out_specs=pl.BlockSpec((tm, tn), lambda i, j: (i, j)),
        out_shape=jax.ShapeDtypeStruct((t, d), F32),
        compiler_params=_params("parallel", "parallel"),
        name="outproj_residual",
    )(a_o, b_o, c_o, w_out, w_out, w_out, h)


def _down_kernel(a_ref, w_ref, h_ref, o_ref):
    @pl.when(pl.program_id(2) == 0)
    def _():
        o_ref[...] = h_ref[...]
    o_ref[...] += jnp.dot(a_ref[...], w_ref[...], preferred_element_type=F32)


def _down_residual(h, a, w):
    t, d = h.shape
    f = a.shape[1]
    tm = _pick_tile(t, 1100, SUBLANES_BF16)
    tn = _pick_tile(d, 2048, LANES)
    tk = _pick_tile(f, 1024, LANES)
    return pl.pallas_call(
        _down_kernel,
        grid=(t // tm, d // tn, f // tk),
        in_specs=[
            pl.BlockSpec((tm, tk), lambda i, j, k: (i, k)),
            pl.BlockSpec((tk, tn), lambda i, j, k: (k, j)),
            pl.BlockSpec((tm, tn), lambda i, j, k: (i, j)),
        ],
        out_specs=pl.BlockSpec((tm, tn), lambda i, j, k: (i, j)),
        out_shape=jax.ShapeDtypeStruct((t, d), F32),
        compiler_params=_params("parallel", "parallel", "arbitrary"),
        name="down_residual",
    )(a, w, h)


def _rmsnorm_kernel(x_ref, g_ref, o_ref):
    x = x_ref[...]
    ms = jnp.mean(x * x, axis=-1, keepdims=True)
    o_ref[...] = x * lax.rsqrt(ms + EPS) * g_ref[...]


def _rmsnorm(x, g):
    t, d = x.shape
    tm = _pick_tile(t, 256, 8)
    return pl.pallas_call(
        _rmsnorm_kernel,
        grid=(t // tm,),
        in_specs=[pl.BlockSpec((tm, d), lambda i: (i, 0)), pl.BlockSpec((1, d), lambda i: (0, 0))],
        out_specs=pl.BlockSpec((tm, d), lambda i: (i, 0)),
        out_shape=jax.ShapeDtypeStruct((t, d), F32),
        compiler_params=_params("parallel"),
        name="final_rmsnorm",
    )(x, g.reshape(1, d))


def _toeplitz_bias(tbl_ref, head, rows, width):
    m = lax.broadcasted_iota(jnp.int32, (8, width), 1)
    idx = jnp.where(m >= BAND + rows, 2 * REL_CLIP, jnp.clip(BAND + REL_CLIP - m, 0, 2 * REL_CLIP))

    def body(d, g):
        return jnp.where(idx == d, tbl_ref[head * TABLE + d], g)

    g = lax.fori_loop(0, TABLE, body, jnp.zeros((8, width), F32))
    full = jnp.broadcast_to(g[0:1, :], (rows, width))
    return pltpu.roll(full, 0, 1, stride=1, stride_axis=0)


def _nt_dot(a, b):
    return lax.dot_general(a, b, (((1,), (1,)), ((), ())), preferred_element_type=F32)


def _attn_prompt_kernel(tbl_ref, q_ref, *refs, nkb):
    k_refs, v_refs = refs[:nkb], refs[nkb:2 * nkb]
    o_ref, bias_ref = refs[2 * nkb], refs[2 * nkb + 1]
    head, b, i = pl.program_id(0), pl.program_id(1), pl.program_id(2)
    tq = q_ref.shape[0]
    tk = nkb * tq

    @pl.when((b == 0) & (i == 0))
    def _():
        bias = _toeplitz_bias(tbl_ref, head, tq, bias_ref.shape[1] + tq)[:, :tk]
        qc = lax.broadcasted_iota(jnp.int32, (tq, tk), 0) // CHUNK
        kc = lax.broadcasted_iota(jnp.int32, (tq, tk), 1) // CHUNK
        bias_ref[...] = jnp.where((kc >= qc) & (kc <= qc + BAND_CHUNKS), bias, NEG_INF)

    q = q_ref[...].astype(BF16)
    scale = HEAD_DIM ** -0.5
    s = []
    for j in range(nkb):
        sj = _nt_dot(q, k_refs[j][...].astype(BF16)) * scale + bias_ref[:, j * tq:(j + 1) * tq]
        if j < nkb - 1:
            sj = jnp.where(i >= nkb - 1 - j, sj, NEG_INF)
        s.append(sj)
    m = functools.reduce(jnp.maximum, [jnp.max(sj, axis=-1, keepdims=True) for sj in s])
    p = [jnp.exp(sj - m) for sj in s]
    l = functools.reduce(jnp.add, [jnp.sum(pj, axis=-1, keepdims=True) for pj in p])
    o = functools.reduce(jnp.add, [jnp.dot(p[j].astype(BF16), v_refs[j][...].astype(BF16),
                                           preferred_element_type=F32) for j in range(nkb)])
    o_ref[...] = (o / l).astype(o_ref.dtype)


def _attn_prompt(z, table, *, batch, seq, heads, t_total):
    tq = ATTN_TQ
    assert seq % tq == 0 and BAND % tq == 0 and tq % CHUNK == 0
    nkb = BAND // tq + 1
    nq = seq // tq
    width = heads * HEAD_DIM

    def kv_spec(col0, j):
        back = nkb - 1 - j
        return pl.BlockSpec((tq, HEAD_DIM), lambda h, b, i: (b * nq + jnp.maximum(i - back, 0), col0 + h))

    in_specs = [pl.BlockSpec(memory_space=pltpu.SMEM),
                pl.BlockSpec((tq, HEAD_DIM), lambda h, b, i: (b * nq + i, h))]
    in_specs += [kv_spec(heads, j) for j in range(nkb)]
    in_specs += [kv_spec(2 * heads, j) for j in range(nkb)]
    return pl.pallas_call(
        functools.partial(_attn_prompt_kernel, nkb=nkb),
        grid=(heads, batch, nq),
        in_specs=in_specs,
        out_specs=pl.BlockSpec((tq, HEAD_DIM), lambda h, b, i: (b * nq + i, h)),
        out_shape=jax.ShapeDtypeStruct((t_total, width), BF16),
        scratch_shapes=[pltpu.VMEM((tq, nkb * tq), F32)],
        compiler_params=_params("arbitrary", "arbitrary", "arbitrary"),
        name="band_attention_prompt",
    )(table.reshape(-1), *([z] * (1 + 2 * nkb)))


def _attn_sample_kernel(tbl_ref, q_ref, kn_ref, vn_ref, kc_ref, vc_ref, prev_ref, o_ref, bias_ref, *, heads):
    del prev_ref
    s_len = q_ref.shape[0]
    la = kc_ref.shape[0]

    @pl.when(pl.program_id(0) == 0)
    def _():
        for h in range(heads):
            bias_ref[h] = _toeplitz_bias(tbl_ref, h, s_len, bias_ref.shape[2])

    scale = HEAD_DIM ** -0.5
    for h in range(heads):
        sl = slice(h * HEAD_DIM, (h + 1) * HEAD_DIM)
        q = q_ref[:, sl].astype(BF16)
        sc = _nt_dot(q, kc_ref[:, sl].astype(BF16)) * scale + bias_ref[h, :, 0:la]
        sn = _nt_dot(q, kn_ref[:, sl].astype(BF16)) * scale + bias_ref[h, :, la:la + s_len]
        m = jnp.maximum(jnp.max(sc, axis=-1, keepdims=True), jnp.max(sn, axis=-1, keepdims=True))
        pc, pn = jnp.exp(sc - m), jnp.exp(sn - m)
        l = jnp.sum(pc, axis=-1, keepdims=True) + jnp.sum(pn, axis=-1, keepdims=True)
        o = jnp.dot(pc.astype(BF16), vc_ref[:, sl].astype(BF16), preferred_element_type=F32)
        o += jnp.dot(pn.astype(BF16), vn_ref[:, sl].astype(BF16), preferred_element_type=F32)
        o_ref[:, sl] = (o / l).astype(o_ref.dtype)


def _attn_sample(z, cache_k, cache_v, table, prev, *, row0, batch, s_len, heads):
    la = cache_k.shape[1]
    assert la == BAND, "the relative-position bias tile assumes a full band of cached rows"
    width = heads * HEAD_DIM
    rb0 = row0 // s_len
    bias_w = pl.cdiv(la + 2 * s_len, LANES) * LANES
    new_spec = lambda c: pl.BlockSpec((s_len, width), lambda b: (rb0 + b, c))
    cache_spec = pl.BlockSpec((None, la, width), lambda b: (b, 0, 0))
    return pl.pallas_call(
        functools.partial(_attn_sample_kernel, heads=heads),
        grid=(batch,),
        in_specs=[pl.BlockSpec(memory_space=pltpu.SMEM), new_spec(0), new_spec(1), new_spec(2),
                  cache_spec, cache_spec, pl.BlockSpec(memory_space=pl.ANY)],
        out_specs=pl.BlockSpec((s_len, width), lambda b: (rb0 + b, 0)),
        out_shape=jax.ShapeDtypeStruct(prev.shape, prev.dtype),
        scratch_shapes=[pltpu.VMEM((heads, s_len, bias_w), F32)],
        input_output_aliases={6: 0},
        compiler_params=_params("arbitrary"),
        name="band_attention_sample",
    )(table.reshape(-1), z, z, z, cache_k, cache_v, prev)


def _ret_kernel(lg_ref, gb_ref, q_ref, k_ref, v_ref, gate_ref, cos_ref, sin_ref, gain_ref, s0_ref, *rest,
                nblk, aliased):
    if aliased:
        rest = rest[1:]
    o_ref, sfin_ref, state_ref, decay_ref = rest
    head, n = pl.program_id(1), pl.program_id(2)
    cb = q_ref.shape[0]
    log_g = lg_ref[head]

    @pl.when(n == 0)
    def _():
        state_ref[...] = s0_ref[...]
        diff = (lax.broadcasted_iota(jnp.int32, (cb, cb), 0) - lax.broadcasted_iota(jnp.int32, (cb, cb), 1))
        diff = diff.astype(F32)
        decay_ref[...] = jnp.where(diff >= 0, jnp.exp(jnp.maximum(diff, 0.0) * log_g), 0.0)

    cos, sin = cos_ref[...], sin_ref[...]

    def rot(x):
        return x * cos + pltpu.roll(x, HEAD_DIM // 2, 1) * sin

    q = rot(q_ref[...])
    k = rot(k_ref[...]) * (HEAD_DIM ** -0.5)
    v = v_ref[...].astype(BF16)
    t = lax.broadcasted_iota(jnp.int32, (cb, HEAD_DIM), 0).astype(F32)
    q_w = jnp.exp((t + 1.0) * log_g)
    k_w = jnp.exp((cb - 1.0 - t) * log_g)

    qb = q.astype(BF16)
    scores = _nt_dot(qb, k.astype(BF16)) * decay_ref[...]
    o = jnp.dot(scores.astype(BF16), v, preferred_element_type=F32)
    o += jnp.dot(qb, state_ref[...].astype(BF16), preferred_element_type=F32) * q_w
    kv = lax.dot_general((k * k_w).astype(BF16), v, (((0,), (0,)), ((), ())), preferred_element_type=F32)
    new_state = state_ref[...] * gb_ref[head] + kv
    state_ref[...] = new_state

    @pl.when(n == nblk - 1)
    def _():
        sfin_ref[...] = new_state

    mu = jnp.mean(o, axis=-1, keepdims=True)
    var = jnp.mean(jnp.square(o - mu), axis=-1, keepdims=True)
    on = (o - mu) * lax.rsqrt(var + EPS) * gain_ref[...]
    gate = gate_ref[...]
    o_ref[...] = (gate * jax.nn.sigmoid(gate) * on).astype(o_ref.dtype)


def _retention(z, s0, gain, pos, prev, *, row0, batch, seq, cb, heads, col0, t_total):
    nblk = seq // cb
    rb0 = row0 // cb
    width = heads * HEAD_DIM
    hidx = jnp.arange(heads, dtype=F32)
    log_g = jnp.log1p(-jnp.exp2(-5.0 - hidx))
    g_block = jnp.exp(cb * log_g)
    half = HEAD_DIM // 2
    inv = ROPE_BASE ** (-jnp.arange(half, dtype=F32) / half)
    ang = pos[:, None] * inv[None, :]
    cos = jnp.concatenate([jnp.cos(ang), jnp.cos(ang)], axis=-1)
    sin = jnp.concatenate([-jnp.sin(ang), jnp.sin(ang)], axis=-1)

    zspec = lambda c: pl.BlockSpec((cb, HEAD_DIM), lambda b, h, n: (rb0 + b * nblk + n, c + h))
    rope_spec = pl.BlockSpec((cb, HEAD_DIM), lambda b, h, n: (n, 0))
    smem = pl.BlockSpec(memory_space=pltpu.SMEM)
    in_specs = [smem, smem, zspec(col0), zspec(col0 + heads), zspec(col0 + 2 * heads), zspec(col0 + 3 * heads),
                rope_spec, rope_spec,
                pl.BlockSpec((1, HEAD_DIM), lambda b, h, n: (0, h)),
                pl.BlockSpec((None, None, HEAD_DIM, HEAD_DIM), lambda b, h, n: (b, h, 0, 0))]
    args = [log_g, g_block, z, z, z, z, cos, sin, gain.reshape(1, width), s0]
    aliases = {}
    if prev is not None:
        in_specs.append(pl.BlockSpec(memory_space=pl.ANY))
        args.append(prev)
        aliases = {len(args) - 1: 0}
    return pl.pallas_call(
        functools.partial(_ret_kernel, nblk=nblk, aliased=prev is not None),
        grid=(batch, heads, nblk),
        in_specs=in_specs,
        out_specs=[pl.BlockSpec((cb, HEAD_DIM), lambda b, h, n: (rb0 + b * nblk + n, h)),
                   pl.BlockSpec((None, None, HEAD_DIM, HEAD_DIM), lambda b, h, n: (b, h, 0, 0))],
        out_shape=[jax.ShapeDtypeStruct((t_total, width), BF16),
                   jax.ShapeDtypeStruct((batch, heads, HEAD_DIM, HEAD_DIM), F32)],
        scratch_shapes=[pltpu.VMEM((HEAD_DIM, HEAD_DIM), F32), pltpu.VMEM((cb, cb), F32)],
        input_output_aliases=aliases,
        compiler_params=_params("arbitrary", "arbitrary", "arbitrary"),
        name="retention_prompt" if prev is None else "retention_sample",
    )(*args)


def _pool_kernel(u_ref, halo_ref, w_ref, sc_ref, *rest, pos0, first_tile_has_no_history, aliased):
    if aliased:
        rest = rest[1:]
    o_ref, x_ref = rest
    i, g = pl.program_id(1), pl.program_id(2)
    tr = u_ref.shape[0]
    halo = halo_ref[...]
    if first_tile_has_no_history:
        halo = jnp.where(i == 0, 0.0, halo)
    x_ref[0:POOL_HALO, :] = halo
    x_ref[POOL_HALO:POOL_HALO + tr, :] = u_ref[...]
    pos = (pos0 + i * tr + lax.broadcasted_iota(jnp.int32, (tr, 1), 0)).astype(F32)

    for gi, w in enumerate(POOL_WINDOWS):
        @pl.when(g == gi)
        def _(w=w):
            win = x_ref[POOL_HALO:POOL_HALO + tr, :]
            for j in range(1, w):
                win = win + x_ref[POOL_HALO - j:POOL_HALO - j + tr, :]
            cnt = jnp.minimum(float(w), pos + 1.0)
            diff = win / cnt - u_ref[...]
            y = jnp.dot(diff.astype(BF16), w_ref[...].astype(BF16), preferred_element_type=F32)
            o_ref[...] = (y * sc_ref[...]).astype(o_ref.dtype)


def _pool(z, halo, w_pool, scale, prev, *, row0, batch, seq, tr, pos0, col0, t_total):
    groups, gw = w_pool.shape[0], w_pool.shape[1]
    assert groups == len(POOL_WINDOWS) and max(POOL_WINDOWS) - 1 <= POOL_HALO and tr % POOL_HALO == 0
    nt = seq // tr
    rb0 = row0 // tr
    in_specs = [pl.BlockSpec((tr, gw), lambda b, i, g: (rb0 + b * nt + i, col0 + g))]
    if halo is None:
        per = tr // POOL_HALO
        in_specs.append(pl.BlockSpec(
            (POOL_HALO, gw), lambda b, i, g: (jnp.maximum((rb0 + b * nt + i) * per - 1, 0), col0 + g)))
        args = [z, z]
    else:
        in_specs.append(pl.BlockSpec((None, POOL_HALO, gw), lambda b, i, g: (b, 0, g)))
        args = [z, halo]
    in_specs += [pl.BlockSpec((None, gw, gw), lambda b, i, g: (g, 0, 0)),
                 pl.BlockSpec((1, gw), lambda b, i, g: (0, g))]
    args += [w_pool, scale.reshape(1, groups * gw)]
    aliases = {}
    if prev is not None:
        in_specs.append(pl.BlockSpec(memory_space=pl.ANY))
        args.append(prev)
        aliases = {len(args) - 1: 0}
    return pl.pallas_call(
        functools.partial(_pool_kernel, pos0=pos0, first_tile_has_no_history=halo is None, aliased=prev is not None),
        grid=(batch, nt, groups),
        in_specs=in_specs,
        out_specs=pl.BlockSpec((tr, gw), lambda b, i, g: (rb0 + b * nt + i, g)),
        out_shape=jax.ShapeDtypeStruct((t_total, groups * gw), BF16),
        scratch_shapes=[pltpu.VMEM((POOL_HALO + tr, gw), F32)],
        input_output_aliases=aliases,
        compiler_params=_params("arbitrary", "arbitrary", "arbitrary"),
        name="pool_prompt" if prev is None else "pool_sample",
    )(*args)


def kernel(x_prompt, x_sample, cache_band_k, cache_band_v, state_retention, state_pool, norm_mix, w_in,
           rel_bias_table, ret_norm, pool_w, pool_scale, w_out, norm_ffn, w_up, w_down, norm_final):
    bp, lp, d = x_prompt.shape
    bs, ls, _ = x_sample.shape
    depth = w_in.shape[0]
    a_heads = rel_bias_table.shape[1]
    a_width = a_heads * HEAD_DIM
    b_width = ret_norm.shape[1]
    b_heads = b_width // HEAD_DIM
    c_width = pool_scale.shape[1]
    gw = pool_w.shape[2]
    la = cache_band_k.shape[2]
    tp, ts = bp * lp, bs * ls
    t_total = tp + ts
    keep = min(BAND, lp)
    pool_ctx = state_pool.shape[2]
    assert a_width % gw == 0 and (3 * a_width + 4 * b_width) % gw == 0
    bq_col = 3 * a_heads
    cu_col = (3 * a_width + 4 * b_width) // gw

    h = jnp.concatenate([x_prompt.reshape(tp, d), x_sample.reshape(ts, d)], axis=0)
    pos_p = jnp.arange(lp, dtype=F32)
    pos_s = PAST_LEN + jnp.arange(ls, dtype=F32)
    zero_state = jnp.zeros((bp, b_heads, HEAD_DIM, HEAD_DIM), F32)
    pool_halo = jnp.pad(state_pool, ((0, 0), (0, 0), (POOL_HALO - pool_ctx, 0), (0, 0)))
    cache_k = cache_band_k.reshape(depth, bs, la, a_width)
    cache_v = cache_band_v.reshape(depth, bs, la, a_width)

    kp_l, vp_l, rp_l, pp_l, ks_l, vs_l, rs_l, ps_l = [], [], [], [], [], [], [], []
    for l in range(depth):
        z = _norm_matmul(h, norm_mix[l], w_in[l].astype(BF16), relu2=False, out_dtype=F32, tn_target=512)

        a_o = _attn_prompt(z, rel_bias_table[l], batch=bp, seq=lp, heads=a_heads, t_total=t_total)
        a_o = _attn_sample(z, cache_k[l], cache_v[l], rel_bias_table[l], a_o,
                           row0=tp, batch=bs, s_len=ls, heads=a_heads)
        b_o, s_fin = _retention(z, zero_state, ret_norm[l], pos_p, None, row0=0, batch=bp, seq=lp,
                                cb=RET_BLOCK, heads=b_heads, col0=bq_col, t_total=t_total)
        b_o, s_new = _retention(z, state_retention[l], ret_norm[l], pos_s, b_o, row0=tp, batch=bs, seq=ls,
                                cb=ls, heads=b_heads, col0=bq_col, t_total=t_total)
        c_o = _pool(z, None, pool_w[l], pool_scale[l], None, row0=0, batch=bp, seq=lp,
                    tr=_pick_tile(lp, 512, POOL_HALO), pos0=0, col0=cu_col, t_total=t_total)
        c_o = _pool(z, pool_halo[l], pool_w[l], pool_scale[l], c_o, row0=tp, batch=bs, seq=ls,
                    tr=ls, pos0=PAST_LEN, col0=cu_col, t_total=t_total)

        h = _outproj(h, a_o, b_o, c_o, w_out[l].astype(BF16))
        act = _norm_matmul(h, norm_ffn[l], w_up[l].astype(BF16), relu2=True, out_dtype=BF16, tn_target=1024)
        h = _down_residual(h, act, w_down[l].astype(BF16))

        zp = z[:tp].reshape(bp, lp, -1)
        zs = z[tp:].reshape(bs, ls, -1)
        kp_l.append(zp[:, lp - keep:, a_width:2 * a_width].reshape(bp, keep, a_heads, HEAD_DIM))
        vp_l.append(zp[:, lp - keep:, 2 * a_width:3 * a_width].reshape(bp, keep, a_heads, HEAD_DIM))
        rp_l.append(s_fin)
        pp_l.append(zp[:, lp - pool_ctx:, cu_col * gw:])
        ks_l.append(zs[:, :, a_width:2 * a_width].reshape(bs, ls, a_heads, HEAD_DIM))
        vs_l.append(zs[:, :, 2 * a_width:3 * a_width].reshape(bs, ls, a_heads, HEAD_DIM))
        rs_l.append(s_new)
        ps_l.append(jnp.concatenate([state_pool[l], zs[:, :, cu_col * gw:]], axis=1)[:, -pool_ctx:])

    y = _rmsnorm(h, norm_final)
    return (y[:tp].reshape(bp, lp, d), y[tp:].reshape(bs, ls, d),
            jnp.stack(kp_l), jnp.stack(vp_l), jnp.stack(rp_l), jnp.stack(pp_l),
            jnp.stack(ks_l), jnp.stack(vs_l), jnp.stack(rs_l), jnp.stack(ps_l))
```

```python
import functools

import jax
import jax.numpy as jnp
from jax import lax
from jax.experimental import pallas as pl
from jax.experimental.pallas import tpu as pltpu

F32 = jnp.float32
BF16 = jnp.bfloat16

PAST_LEN = 1024
CHUNK = 64
BAND_CHUNKS = 8
BAND = BAND_CHUNKS * CHUNK
HEAD_DIM = 128
REL_CLIP = 128
TABLE = 2 * REL_CLIP + 1
POOL_WINDOWS = (2, 4, 8, 16)
POOL_HALO = 16
ROPE_BASE = 10000.0
EPS = 1e-6
NEG_INF = -1e30

LANES = 128
SUBLANES_BF16 = 16
VMEM_LIMIT_BYTES = 60 * 1024 * 1024

ATTN_TQ = 256
RET_BLOCK = 256
NORM_ROWS = 64
ROW_TILE = 1100


def _pick_tile(n, target, align):
    best = None
    for t in range(align, min(n, target) + 1, align):
        if n % t == 0:
            best = t
    if best is None:
        raise ValueError(f"no tile for {n} (align {align}, target {target})")
    return best


def _params(*sem):
    return pltpu.CompilerParams(dimension_semantics=sem, vmem_limit_bytes=VMEM_LIMIT_BYTES)


def _norm_rows(x_ref, g_ref, xn_ref):
    def body(c, carry):
        r = pl.multiple_of(c * NORM_ROWS, NORM_ROWS)
        x = x_ref[pl.ds(r, NORM_ROWS), :]
        ms = jnp.mean(x * x, axis=-1, keepdims=True)
        xn_ref[pl.ds(r, NORM_ROWS), :] = (x * lax.rsqrt(ms + EPS) * g_ref[...]).astype(BF16)
        return carry
    lax.fori_loop(0, x_ref.shape[0] // NORM_ROWS, body, 0)


def _norm_matmul_kernel(x_ref, g_ref, w_ref, o_ref, xn_ref, *, relu2):
    @pl.when(pl.program_id(1) == 0)
    def _():
        _norm_rows(x_ref, g_ref, xn_ref)
    y = jnp.dot(xn_ref[...], w_ref[...].astype(BF16), preferred_element_type=F32)
    if relu2:
        y = jnp.square(jnp.maximum(y, 0.0))
    o_ref[...] = y.astype(o_ref.dtype)


def _norm_matmul(x, g, w, layer, *, relu2, out_dtype):
    t, k = x.shape
    n = w.shape[2]
    tm = _pick_tile(t, ROW_TILE, NORM_ROWS)
    tn = _pick_tile(n, 512, LANES)
    return pl.pallas_call(
        functools.partial(_norm_matmul_kernel, relu2=relu2),
        grid=(t // tm, n // tn),
        in_specs=[
            pl.BlockSpec((tm, k), lambda i, j: (i, 0), pipeline_mode=pl.Buffered(1)),
            pl.BlockSpec((None, 1, k), lambda i, j: (layer, 0, 0)),
            pl.BlockSpec((None, k, tn), lambda i, j: (layer, 0, j)),
        ],
        out_specs=pl.BlockSpec((tm, tn), lambda i, j: (i, j)),
        out_shape=jax.ShapeDtypeStruct((t, n), out_dtype),
        scratch_shapes=[pltpu.VMEM((tm, k), BF16)],
        compiler_params=_params("parallel", "arbitrary"),
        name="norm_matmul_relu2" if relu2 else "norm_matmul",
    )(x, g, w)


def _outproj_kernel(a_ref, b_ref, c_ref, wa_ref, wb_ref, wc_ref, h_ref, o_ref):
    acc = jnp.dot(a_ref[...], wa_ref[...].astype(BF16), preferred_element_type=F32)
    acc += jnp.dot(b_ref[...], wb_ref[...].astype(BF16), preferred_element_type=F32)
    acc += jnp.dot(c_ref[...], wc_ref[...].astype(BF16), preferred_element_type=F32)
    o_ref[...] = h_ref[...] + acc


def _outproj(h, a_o, b_o, c_o, w_out, layer):
    t, d = h.shape
    wa, wb, wc = a_o.shape[1], b_o.shape[1], c_o.shape[1]
    assert wa == wb and (wa + wb) % wc == 0
    tm = _pick_tile(t, ROW_TILE, SUBLANES_BF16)
    tn = _pick_tile(d, 512, LANES)
    return pl.pallas_call(
        _outproj_kernel,
        grid=(t // tm, d // tn),
        in_specs=[
            pl.BlockSpec((tm, wa), lambda i, j: (i, 0)),
            pl.BlockSpec((tm, wb), lambda i, j: (i, 0)),
            pl.BlockSpec((tm, wc), lambda i, j: (i, 0)),
            pl.BlockSpec((None, wa, tn), lambda i, j: (layer, 0, j)),
            pl.BlockSpec((None, wb, tn), lambda i, j: (layer, 1, j)),
            pl.BlockSpec((None, wc, tn), lambda i, j: (layer, (wa + wb) // wc, j)),
            pl.BlockSpec((tm, tn), lambda i, j: (i, j)),
        ],
        out_specs=pl.BlockSpec((tm, tn), lambda i, j: (i, j)),
        out_shape=jax.ShapeDtypeStruct((t, d), F32),
        compiler_params=_params("parallel", "parallel"),
        name="outproj_residual",
    )(a_o, b_o, c_o, w_out, w_out, w_out, h)


def _down_kernel(a_ref, w_ref, h_ref, o_ref):
    @pl.when(pl.program_id(2) == 0)
    def _():
        o_ref[...] = h_ref[...]
    o_ref[...] += jnp.dot(a_ref[...], w_ref[...].astype(BF16), preferred_element_type=F32)


def _down_residual(h, a, w, layer):
    t, d = h.shape
    f = a.shape[1]
    tm = _pick_tile(t, ROW_TILE, SUBLANES_BF16)
    tn = _pick_tile(d, 2048, LANES)
    tk = _pick_tile(f, 1024, LANES)
    return pl.pallas_call(
        _down_kernel,
        grid=(t // tm, d // tn, f // tk),
        in_specs=[
            pl.BlockSpec((tm, tk), lambda i, j, k: (i, k)),
            pl.BlockSpec((None, tk, tn), lambda i, j, k: (layer, k, j)),
            pl.BlockSpec((tm, tn), lambda i, j, k: (i, j), pipeline_mode=pl.Buffered(1)),
        ],
        out_specs=pl.BlockSpec((tm, tn), lambda i, j, k: (i, j)),
        out_shape=jax.ShapeDtypeStruct((t, d), F32),
        compiler_params=_params("parallel", "parallel", "arbitrary"),
        name="down_residual",
    )(a, w, h)


def _rmsnorm_kernel(x_ref, g_ref, o_ref):
    x = x_ref[...]
    ms = jnp.mean(x * x, axis=-1, keepdims=True)
    o_ref[...] = x * lax.rsqrt(ms + EPS) * g_ref[...]


def _rmsnorm(x, g, *, row0, rows):
    d = x.shape[1]
    tm = _pick_tile(rows, 256, 8)
    assert row0 % tm == 0
    return pl.pallas_call(
        _rmsnorm_kernel,
        grid=(rows // tm,),
        in_specs=[pl.BlockSpec((tm, d), lambda i: (row0 // tm + i, 0)), pl.BlockSpec((1, d), lambda i: (0, 0))],
        out_specs=pl.BlockSpec((tm, d), lambda i: (i, 0)),
        out_shape=jax.ShapeDtypeStruct((rows, d), F32),
        compiler_params=_params("parallel"),
        name="final_rmsnorm",
    )(x, g.reshape(1, d))


def _toeplitz_bias(tbl_ref, base, rows, width):
    m = lax.broadcasted_iota(jnp.int32, (8, width), 1)
    idx = jnp.where(m >= BAND + rows, 2 * REL_CLIP, jnp.clip(BAND + REL_CLIP - m, 0, 2 * REL_CLIP))

    def body(d, g):
        return jnp.where(idx == d, tbl_ref[base + d], g)

    g = lax.fori_loop(0, TABLE, body, jnp.zeros((8, width), F32))
    full = jnp.broadcast_to(g[0:1, :], (rows, width))
    return pltpu.roll(full, 0, 1, stride=1, stride_axis=0)


def _nt_dot(a, b):
    return lax.dot_general(a, b, (((1,), (1,)), ((), ())), preferred_element_type=F32)


def _attn_prompt_kernel(tbl_ref, q_ref, k_ref, v_ref, o_ref, bias_ref, kb_ref, vb_ref, *, layer, heads, tq, nkb):
    head, b = pl.program_id(0), pl.program_id(1)
    seq = q_ref.shape[0]
    tk = nkb * tq

    @pl.when(b == 0)
    def _():
        bias = _toeplitz_bias(tbl_ref, (layer * heads + head) * TABLE, tq, tk + tq)[:, :tk]
        qc = lax.broadcasted_iota(jnp.int32, (tq, tk), 0) // CHUNK
        kc = lax.broadcasted_iota(jnp.int32, (tq, tk), 1) // CHUNK
        bias_ref[...] = jnp.where((kc >= qc) & (kc <= qc + BAND_CHUNKS), bias, NEG_INF)

    kb_ref[...] = k_ref[...].astype(BF16)
    vb_ref[...] = v_ref[...].astype(BF16)
    scale = HEAD_DIM ** -0.5

    def tile(q0, k0, nk, bias):
        q = q_ref[pl.ds(q0, tq), :].astype(BF16)
        s = _nt_dot(q, kb_ref[pl.ds(k0, nk), :]) * scale + bias
        m = jnp.max(s, axis=-1, keepdims=True)
        p = jnp.exp(s - m)
        l = jnp.sum(p, axis=-1, keepdims=True)
        o = jnp.dot(p.astype(BF16), vb_ref[pl.ds(k0, nk), :], preferred_element_type=F32)
        o_ref[pl.ds(q0, tq), :] = (o / l).astype(o_ref.dtype)

    for i in range(nkb - 1):
        nk = (i + 1) * tq
        tile(i * tq, 0, nk, bias_ref[:, tk - nk:tk])

    def body(i, carry):
        q0 = pl.multiple_of(i * tq, tq)
        k0 = pl.multiple_of((i - (nkb - 1)) * tq, tq)
        tile(q0, k0, tk, bias_ref[...])
        return carry

    lax.fori_loop(nkb - 1, seq // tq, body, 0)


def _attn_prompt(z, table, layer, *, batch, seq, heads, t_total):
    tq = ATTN_TQ
    assert seq % tq == 0 and BAND % tq == 0 and tq % CHUNK == 0 and seq // tq >= BAND // tq + 1
    nkb = BAND // tq + 1
    width = heads * HEAD_DIM
    spec = lambda c: pl.BlockSpec((seq, HEAD_DIM), lambda h, b: (b, c + h))
    return pl.pallas_call(
        functools.partial(_attn_prompt_kernel, layer=layer, heads=heads, tq=tq, nkb=nkb),
        grid=(heads, batch),
        in_specs=[pl.BlockSpec(memory_space=pltpu.SMEM), spec(0), spec(heads), spec(2 * heads)],
        out_specs=pl.BlockSpec((seq, HEAD_DIM), lambda h, b: (b, h)),
        out_shape=jax.ShapeDtypeStruct((t_total, width), BF16),
        scratch_shapes=[pltpu.VMEM((tq, nkb * tq), F32), pltpu.VMEM((seq, HEAD_DIM), BF16),
                        pltpu.VMEM((seq, HEAD_DIM), BF16)],
        compiler_params=_params("arbitrary", "arbitrary"),
        name="band_attention_prompt",
    )(table.reshape(-1), z, z, z)


def _attn_sample_kernel(tbl_ref, q_ref, kn_ref, vn_ref, kc_ref, vc_ref, prev_ref, o_ref, bias_ref, *, layer, heads):
    del prev_ref
    s_len = q_ref.shape[0]
    la = kc_ref.shape[0]

    @pl.when(pl.program_id(0) == 0)
    def _():
        for h in range(heads):
            bias_ref[h] = _toeplitz_bias(tbl_ref, (layer * heads + h) * TABLE, s_len, bias_ref.shape[2])

    scale = HEAD_DIM ** -0.5
    for h in range(heads):
        sl = slice(h * HEAD_DIM, (h + 1) * HEAD_DIM)
        q = q_ref[:, sl].astype(BF16)
        sc = _nt_dot(q, kc_ref[:, h, :].astype(BF16)) * scale + bias_ref[h, :, 0:la]
        sn = _nt_dot(q, kn_ref[:, sl].astype(BF16)) * scale + bias_ref[h, :, la:la + s_len]
        m = jnp.maximum(jnp.max(sc, axis=-1, keepdims=True), jnp.max(sn, axis=-1, keepdims=True))
        pc, pn = jnp.exp(sc - m), jnp.exp(sn - m)
        l = jnp.sum(pc, axis=-1, keepdims=True) + jnp.sum(pn, axis=-1, keepdims=True)
        o = jnp.dot(pc.astype(BF16), vc_ref[:, h, :].astype(BF16), preferred_element_type=F32)
        o += jnp.dot(pn.astype(BF16), vn_ref[:, sl].astype(BF16), preferred_element_type=F32)
        o_ref[:, sl] = (o / l).astype(o_ref.dtype)


def _attn_sample(z, cache_k, cache_v, table, prev, layer, *, row0, batch, s_len, heads):
    la = cache_k.shape[2]
    assert la == BAND, "the relative-position bias tile assumes a full band of cached rows"
    width = heads * HEAD_DIM
    rb0 = row0 // s_len
    bias_w = pl.cdiv(la + 2 * s_len, LANES) * LANES
    new_spec = lambda c: pl.BlockSpec((s_len, width), lambda b: (rb0 + b, c))
    cache_spec = pl.BlockSpec((None, None, la, heads, HEAD_DIM), lambda b: (layer, b, 0, 0, 0))
    return pl.pallas_call(
        functools.partial(_attn_sample_kernel, layer=layer, heads=heads),
        grid=(batch,),
        in_specs=[pl.BlockSpec(memory_space=pltpu.SMEM), new_spec(0), new_spec(1), new_spec(2),
                  cache_spec, cache_spec, pl.BlockSpec(memory_space=pl.ANY)],
        out_specs=pl.BlockSpec((s_len, width), lambda b: (rb0 + b, 0)),
        out_shape=jax.ShapeDtypeStruct(prev.shape, prev.dtype),
        scratch_shapes=[pltpu.VMEM((heads, s_len, bias_w), F32)],
        input_output_aliases={6: 0},
        compiler_params=_params("arbitrary"),
        name="band_attention_sample",
    )(table.reshape(-1), z, z, z, cache_k, cache_v, prev)


def _ret_kernel(lg_ref, gb_ref, q_ref, k_ref, v_ref, gate_ref, cos_ref, sin_ref, gain_ref, s0_ref, *rest,
                cb, nblk, hp, aliased):
    if aliased:
        rest = rest[1:]
    o_ref, sfin_ref = rest[0], rest[1]
    hgroup = pl.program_id(1)
    row = lax.broadcasted_iota(jnp.int32, (cb, cb), 0)
    col = lax.broadcasted_iota(jnp.int32, (cb, cb), 1)
    diff = (row - col).astype(F32)
    t = lax.broadcasted_iota(jnp.int32, (cb, HEAD_DIM), 0).astype(F32)

    for hh in range(hp):
        head = hgroup * hp + hh
        log_g = lg_ref[head]
        g_block = gb_ref[head]
        sl = slice(hh * HEAD_DIM, (hh + 1) * HEAD_DIM)
        decay = jnp.where(diff >= 0, jnp.exp(jnp.maximum(diff, 0.0) * log_g), 0.0)
        if nblk > 1:
            decay_ref = rest[2]
            decay_ref[...] = decay
        q_w = jnp.exp((t + 1.0) * log_g)
        k_w = jnp.exp((cb - 1.0 - t) * log_g)
        gain = gain_ref[:, sl]

        def block(n, state, sl=sl, q_w=q_w, k_w=k_w, gain=gain, decay=decay):
            r = pl.multiple_of(n * cb, cb)
            rows = pl.ds(r, cb)
            cos, sin = cos_ref[rows, :], sin_ref[rows, :]

            def rot(x):
                return x * cos + pltpu.roll(x, HEAD_DIM // 2, 1) * sin

            q = rot(q_ref[rows, sl])
            k = rot(k_ref[rows, sl]) * (HEAD_DIM ** -0.5)
            v = v_ref[rows, sl].astype(BF16)
            qb = q.astype(BF16)
            dec = rest[2][...] if nblk > 1 else decay
            scores = _nt_dot(qb, k.astype(BF16)) * dec
            o = jnp.dot(scores.astype(BF16), v, preferred_element_type=F32)
            o += jnp.dot(qb, state.astype(BF16), preferred_element_type=F32) * q_w
            kv = lax.dot_general((k * k_w).astype(BF16), v, (((0,), (0,)), ((), ())), preferred_element_type=F32)
            mu = jnp.mean(o, axis=-1, keepdims=True)
            var = jnp.mean(jnp.square(o - mu), axis=-1, keepdims=True)
            on = (o - mu) * lax.rsqrt(var + EPS) * gain
            gate = gate_ref[rows, sl]
            o_ref[rows, sl] = (gate * jax.nn.sigmoid(gate) * on).astype(o_ref.dtype)
            return state * g_block + kv

        state = s0_ref[hh]
        if nblk > 1:
            state = lax.fori_loop(0, nblk, block, state)
        else:
            state = block(0, state)
        sfin_ref[hh] = state


def _retention(z, s0, gain, pos, prev, layer, *, row0, batch, seq, cb, hp, heads, col0, t_total):
    nblk = seq // cb
    assert seq % cb == 0 and heads % hp == 0 and row0 % seq == 0 and col0 % hp == 0
    rb0 = row0 // seq
    width = heads * HEAD_DIM
    hidx = jnp.arange(heads, dtype=F32)
    log_g = jnp.log1p(-jnp.exp2(-5.0 - hidx))
    g_block = jnp.exp(cb * log_g)
    half = HEAD_DIM // 2
    inv = ROPE_BASE ** (-jnp.arange(half, dtype=F32) / half)
    ang = pos[:, None] * inv[None, :]
    cos = jnp.concatenate([jnp.cos(ang), jnp.cos(ang)], axis=-1)
    sin = jnp.concatenate([-jnp.sin(ang), jnp.sin(ang)], axis=-1)

    bw = hp * HEAD_DIM
    zspec = lambda c: pl.BlockSpec((seq, bw), lambda b, h: (rb0 + b, (c + h * hp) // hp))
    rope_spec = pl.BlockSpec((seq, HEAD_DIM), lambda b, h: (0, 0))
    smem = pl.BlockSpec(memory_space=pltpu.SMEM)
    if s0.ndim == 5:
        s0_spec = pl.BlockSpec((None, None, hp, HEAD_DIM, HEAD_DIM), lambda b, h: (layer, b, h, 0, 0))
    else:
        s0_spec = pl.BlockSpec((None, hp, HEAD_DIM, HEAD_DIM), lambda b, h: (b, h, 0, 0))
    gain_spec = pl.BlockSpec((None, 1, bw), lambda b, h: (layer, 0, h))
    in_specs = [smem, smem, zspec(col0), zspec(col0 + heads), zspec(col0 + 2 * heads), zspec(col0 + 3 * heads),
                rope_spec, rope_spec, gain_spec, s0_spec]
    args = [log_g, g_block, z, z, z, z, cos, sin, gain, s0]
    aliases = {}
    if prev is not None:
        in_specs.append(pl.BlockSpec(memory_space=pl.ANY))
        args.append(prev)
        aliases = {len(args) - 1: 0}
    return pl.pallas_call(
        functools.partial(_ret_kernel, cb=cb, nblk=nblk, hp=hp, aliased=prev is not None),
        grid=(batch, heads // hp),
        in_specs=in_specs,
        out_specs=[pl.BlockSpec((seq, bw), lambda b, h: (rb0 + b, h)),
                   pl.BlockSpec((None, hp, HEAD_DIM, HEAD_DIM), lambda b, h: (b, h, 0, 0))],
        out_shape=[jax.ShapeDtypeStruct((t_total, width), BF16),
                   jax.ShapeDtypeStruct((batch, heads, HEAD_DIM, HEAD_DIM), F32)],
        scratch_shapes=[pltpu.VMEM((cb, cb), F32)] if nblk > 1 else [],
        input_output_aliases=aliases,
        compiler_params=_params("arbitrary", "arbitrary"),
        name="retention_prompt" if prev is None else "retention_sample",
    )(*args)


def _pool_kernel(u_ref, halo_ref, w_ref, sc_ref, *rest, pos0, nseq, first_tile_has_no_history, aliased):
    if aliased:
        rest = rest[1:]
    o_ref, x_ref = rest
    i, g = pl.program_id(1), pl.program_id(2)
    tr = u_ref.shape[0]
    sr = tr // nseq
    for s in range(nseq):
        halo = halo_ref[s] if nseq > 1 else halo_ref[...]
        if first_tile_has_no_history:
            halo = jnp.where(i == 0, 0.0, halo)
        x_ref[s, 0:POOL_HALO, :] = halo
        x_ref[s, POOL_HALO:POOL_HALO + sr, :] = u_ref[s * sr:(s + 1) * sr, :]
    pos = (pos0 + i * sr + lax.broadcasted_iota(jnp.int32, (sr, 1), 0)).astype(F32)

    for gi, w in enumerate(POOL_WINDOWS):
        @pl.when(g == gi)
        def _(w=w):
            inv_cnt = 1.0 / jnp.minimum(float(w), pos + 1.0)
            parts = []
            for s in range(nseq):
                win = x_ref[s, POOL_HALO:POOL_HALO + sr, :]
                for j in range(1, w):
                    win = win + x_ref[s, POOL_HALO - j:POOL_HALO - j + sr, :]
                parts.append(win * inv_cnt - x_ref[s, POOL_HALO:POOL_HALO + sr, :])
            diff = parts[0] if nseq == 1 else jnp.concatenate(parts, axis=0)
            y = jnp.dot(diff.astype(BF16), w_ref[...].astype(BF16), preferred_element_type=F32)
            o_ref[...] = (y * sc_ref[...]).astype(o_ref.dtype)


def _pool(z, halo, w_pool, scale, prev, layer, *, row0, batch, seq, tr, pos0, col0, t_total):
    groups, gw = w_pool.shape[1], w_pool.shape[2]
    assert groups == len(POOL_WINDOWS) and max(POOL_WINDOWS) - 1 <= POOL_HALO and tr % POOL_HALO == 0
    in_specs = []
    if halo is None:
        nseq, nt, nb = 1, seq // tr, batch
        rb0 = row0 // tr
        per = tr // POOL_HALO
        in_specs += [pl.BlockSpec((tr, gw), lambda b, i, g: (rb0 + b * nt + i, col0 + g)),
                     pl.BlockSpec((POOL_HALO, gw),
                                  lambda b, i, g: (jnp.maximum((rb0 + b * nt + i) * per - 1, 0), col0 + g))]
        args = [z, z]
    else:
        assert tr == batch * seq and row0 % tr == 0
        nseq, nt, nb = batch, 1, 1
        rb0 = row0 // tr
        in_specs += [pl.BlockSpec((tr, gw), lambda b, i, g: (rb0, col0 + g)),
                     pl.BlockSpec((None, batch, POOL_HALO, gw), lambda b, i, g: (layer, 0, 0, g))]
        args = [z, halo]
    in_specs += [pl.BlockSpec((None, None, gw, gw), lambda b, i, g: (layer, g, 0, 0)),
                 pl.BlockSpec((None, 1, gw), lambda b, i, g: (layer, 0, g))]
    args += [w_pool, scale]
    aliases = {}
    if prev is not None:
        in_specs.append(pl.BlockSpec(memory_space=pl.ANY))
        args.append(prev)
        aliases = {len(args) - 1: 0}
    return pl.pallas_call(
        functools.partial(_pool_kernel, pos0=pos0, nseq=nseq, first_tile_has_no_history=halo is None,
                          aliased=prev is not None),
        grid=(nb, nt, groups),
        in_specs=in_specs,
        out_specs=pl.BlockSpec((tr, gw), lambda b, i, g: (rb0 + b * nt + i, g)),
        out_shape=jax.ShapeDtypeStruct((t_total, groups * gw), BF16),
        scratch_shapes=[pltpu.VMEM((nseq, POOL_HALO + tr // nseq, gw), F32)],
        input_output_aliases=aliases,
        compiler_params=_params("arbitrary", "arbitrary", "arbitrary"),
        name="pool_prompt" if prev is None else "pool_sample",
    )(*args)


def kernel(x_prompt, x_sample, cache_band_k, cache_band_v, state_retention, state_pool, norm_mix, w_in,
           rel_bias_table, ret_norm, pool_w, pool_scale, w_out, norm_ffn, w_up, w_down, norm_final):
    bp, lp, d = x_prompt.shape
    bs, ls, _ = x_sample.shape
    depth = w_in.shape[0]
    a_heads = rel_bias_table.shape[1]
    a_width = a_heads * HEAD_DIM
    b_width = ret_norm.shape[1]
    b_heads = b_width // HEAD_DIM
    gw = pool_w.shape[2]
    tp, ts = bp * lp, bs * ls
    t_total = tp + ts
    keep = min(BAND, lp)
    pool_ctx = state_pool.shape[2]
    assert a_width % gw == 0 and (3 * a_width + 4 * b_width) % gw == 0 and tp % ts == 0
    bq_col = 3 * a_heads
    cu_col = (3 * a_width + 4 * b_width) // gw
    cu0 = cu_col * gw

    h = jnp.concatenate([x_prompt.reshape(tp, d), x_sample.reshape(ts, d)], axis=0)
    pos_p = jnp.arange(lp, dtype=F32)
    pos_s = PAST_LEN + jnp.arange(ls, dtype=F32)
    zero_state = jnp.zeros((bp, b_heads, HEAD_DIM, HEAD_DIM), F32)
    pool_halo = jnp.pad(state_pool, ((0, 0), (0, 0), (POOL_HALO - pool_ctx, 0), (0, 0)))
    norm_mix, norm_ffn, ret_norm, pool_scale = (
        g.reshape(depth, 1, -1) for g in (norm_mix, norm_ffn, ret_norm, pool_scale))

    def tail(z, b, rows, c0, c1):
        return lax.slice(z, ((b + 1) * lp - rows, c0), ((b + 1) * lp, c1))

    kp_l, vp_l, rp_l, pp_l, ks_l, vs_l, rs_l, ps_l = [], [], [], [], [], [], [], []
    for l in range(depth):
        z = _norm_matmul(h, norm_mix, w_in, l, relu2=False, out_dtype=F32)

        a_o = _attn_prompt(z, rel_bias_table, l, batch=bp, seq=lp, heads=a_heads, t_total=t_total)
        a_o = _attn_sample(z, cache_band_k, cache_band_v, rel_bias_table, a_o, l,
                           row0=tp, batch=bs, s_len=ls, heads=a_heads)
        b_o, s_fin = _retention(z, zero_state, ret_norm, pos_p, None, l, row0=0, batch=bp, seq=lp,
                                cb=RET_BLOCK, hp=1, heads=b_heads, col0=bq_col, t_total=t_total)
        b_o, s_new = _retention(z, state_retention, ret_norm, pos_s, b_o, l, row0=tp, batch=bs, seq=ls,
                                cb=ls, hp=b_heads, heads=b_heads, col0=bq_col, t_total=t_total)
        c_o = _pool(z, None, pool_w, pool_scale, None, l, row0=0, batch=bp, seq=lp,
                    tr=_pick_tile(lp, 512, POOL_HALO), pos0=0, col0=cu_col, t_total=t_total)
        c_o = _pool(z, pool_halo, pool_w, pool_scale, c_o, l, row0=tp, batch=bs, seq=ls,
                    tr=ts, pos0=PAST_LEN, col0=cu_col, t_total=t_total)

        h = _outproj(h, a_o, b_o, c_o, w_out, l)
        act = _norm_matmul(h, norm_ffn, w_up, l, relu2=True, out_dtype=BF16)
        h = _down_residual(h, act, w_down, l)

        kp_l.append(jnp.stack([tail(z, b, keep, a_width, 2 * a_width) for b in range(bp)])
                    .reshape(bp, keep, a_heads, HEAD_DIM))
        vp_l.append(jnp.stack([tail(z, b, keep, 2 * a_width, 3 * a_width) for b in range(bp)])
                    .reshape(bp, keep, a_heads, HEAD_DIM))
        rp_l.append(s_fin)
        pp_l.append(jnp.stack([tail(z, b, pool_ctx, cu0, z.shape[1]) for b in range(bp)]))
        zs_kv = lax.slice(z, (tp, a_width), (t_total, 3 * a_width)).reshape(bs, ls, 2, a_heads, HEAD_DIM)
        ks_l.append(zs_kv[:, :, 0])
        vs_l.append(zs_kv[:, :, 1])
        rs_l.append(s_new)
        cu_s = lax.slice(z, (tp, cu0), (t_total, z.shape[1])).reshape(bs, ls, -1)
        ps_l.append(jnp.concatenate([state_pool[l], cu_s], axis=1)[:, -pool_ctx:])

    y_prompt = _rmsnorm(h, norm_final, row0=0, rows=tp)
    y_sample = _rmsnorm(h, norm_final, row0=tp, rows=ts)
    return (y_prompt.reshape(bp, lp, d), y_sample.reshape(bs, ls, d),
            jnp.stack(kp_l), jnp.stack(vp_l), jnp.stack(rp_l), jnp.stack(pp_l),
            jnp.stack(ks_l), jnp.stack(vs_l), jnp.stack(rs_l), jnp.stack(ps_l))
```

```python
import functools
import math

import jax
import jax.numpy as jnp
from jax import lax
from jax.experimental import pallas as pl
from jax.experimental.pallas import tpu as pltpu

F32 = jnp.float32
BF16 = jnp.bfloat16

PAST_LEN = 1024
CHUNK = 64
BAND_CHUNKS = 8
BAND = BAND_CHUNKS * CHUNK
HEAD_DIM = 128
REL_CLIP = 128
TABLE = 2 * REL_CLIP + 1
POOL_WINDOWS = (2, 4, 8, 16)
POOL_HALO = 16
ROPE_BASE = 10000.0
EPS = 1e-6
NEG_INF = -1e30
LOG2_E = 1.4426950408889634

LANES = 128
SUBLANES_BF16 = 16
MXU_COLS = 256
VMEM_LIMIT_BYTES = 60 * 1024 * 1024

ATTN_TQ = 256
RET_BLOCK = 256
LOOP_UNROLL = 2
STAGE_ROWS = 256
ROW_TILE = 1100
WIDE_ROW_TILE = 2200
POOL_ROWS = 2048


def _pick_tile(n, target, align):
    best = None
    for t in range(align, min(n, target) + 1, align):
        if n % t == 0:
            best = t
    if best is None:
        raise ValueError(f"no tile for {n} (align {align}, target {target})")
    return best


def _params(*sem):
    return pltpu.CompilerParams(dimension_semantics=sem, vmem_limit_bytes=VMEM_LIMIT_BYTES)


def _fold_lanes(y):
    return functools.reduce(jnp.add, [y[:, c:c + LANES] for c in range(0, y.shape[1], LANES)])


def _stage_kernel(*refs, n_first, copy):
    n_out = 3 if copy else 2
    x_refs, g_ref, outs = refs[:-n_out - 1], refs[-n_out - 1], refs[-n_out:]

    def emit(x_ref):
        x = x_ref[...]
        if copy:
            outs[0][...] = x
        outs[-2][...] = (x * g_ref[...]).astype(BF16)
        outs[-1][...] = _fold_lanes(x * x)

    if len(x_refs) == 1:
        emit(x_refs[0])
    else:
        @pl.when(pl.program_id(0) < n_first)
        def _():
            emit(x_refs[0])

        @pl.when(pl.program_id(0) >= n_first)
        def _():
            emit(x_refs[1])


def _stage(xs, g, layer):
    d = xs[0].shape[1]
    tm = _pick_tile(math.gcd(*[x.shape[0] for x in xs]), STAGE_ROWS, SUBLANES_BF16)
    n_first = xs[0].shape[0] // tm
    t = sum(x.shape[0] for x in xs)
    copy = len(xs) > 1
    in_specs = [pl.BlockSpec((tm, d), lambda i: (jnp.minimum(i, n_first - 1), 0))]
    if copy:
        in_specs.append(pl.BlockSpec((tm, d), lambda i: (jnp.maximum(i - n_first, 0), 0)))
    in_specs.append(pl.BlockSpec((None, 1, d), lambda i: (layer, 0, 0)))
    out_specs = [pl.BlockSpec((tm, d), lambda i: (i, 0)), pl.BlockSpec((tm, LANES), lambda i: (i, 0))]
    out_shape = [jax.ShapeDtypeStruct((t, d), BF16), jax.ShapeDtypeStruct((t, LANES), F32)]
    if copy:
        out_specs.insert(0, pl.BlockSpec((tm, d), lambda i: (i, 0)))
        out_shape.insert(0, jax.ShapeDtypeStruct((t, d), F32))
    return pl.pallas_call(
        functools.partial(_stage_kernel, n_first=n_first, copy=copy),
        grid=(t // tm,),
        in_specs=in_specs,
        out_specs=out_specs,
        out_shape=out_shape,
        compiler_params=_params("parallel"),
        name="stage_rows",
    )(*xs, g)


def _scaled_matmul_kernel(hg_ref, ssq_ref, w_ref, o_ref, r_ref, *, relu2, row_parts):
    @pl.when(pl.program_id(1) == 0)
    def _():
        ms = jnp.sum(ssq_ref[...], axis=-1, keepdims=True) * (1.0 / hg_ref.shape[1])
        r_ref[...] = jnp.broadcast_to(lax.rsqrt(ms + EPS), r_ref.shape)
    tm = o_ref.shape[0]
    rt = tm // row_parts
    for c in range(0, o_ref.shape[1], MXU_COLS):
        wc = w_ref[:, c:c + MXU_COLS].astype(BF16)
        for r0 in range(0, tm, rt):
            y = jnp.dot(hg_ref[r0:r0 + rt, :], wc, preferred_element_type=F32)
            for cc in range(0, MXU_COLS, LANES):
                yc = y[:, cc:cc + LANES] * r_ref[r0:r0 + rt, :]
                if relu2:
                    yc = jnp.square(jnp.maximum(yc, 0.0))
                o_ref[r0:r0 + rt, c + cc:c + cc + LANES] = yc.astype(o_ref.dtype)


def _scaled_matmul(hg, ssq, w, layer, *, relu2, out_dtype):
    t, k = hg.shape
    n = w.shape[2]
    tm = _pick_tile(t, WIDE_ROW_TILE, SUBLANES_BF16)
    tn = _pick_tile(n, 512, MXU_COLS)
    row_parts = 2 if tm % (2 * SUBLANES_BF16) == 0 else 1
    return pl.pallas_call(
        functools.partial(_scaled_matmul_kernel, relu2=relu2, row_parts=row_parts),
        grid=(t // tm, n // tn),
        in_specs=[
            pl.BlockSpec((tm, k), lambda i, j: (i, 0), pipeline_mode=pl.Buffered(1)),
            pl.BlockSpec((tm, LANES), lambda i, j: (i, 0)),
            pl.BlockSpec((None, k, tn), lambda i, j: (layer, 0, j)),
        ],
        out_specs=pl.BlockSpec((tm, tn), lambda i, j: (i, j)),
        out_shape=jax.ShapeDtypeStruct((t, n), out_dtype),
        scratch_shapes=[pltpu.VMEM((tm, LANES), F32)],
        compiler_params=_params("parallel", "arbitrary"),
        name="scaled_matmul_relu2" if relu2 else "scaled_matmul",
    )(hg, ssq, w)


def _outproj_kernel(a_ref, b_ref, c_ref, wa_ref, wb_ref, wc_ref, h_ref, g_ref, o_ref, hg_ref, ssq_ref):
    ssq = None
    for c in range(0, o_ref.shape[1], MXU_COLS):
        cols = slice(c, c + MXU_COLS)
        acc = jnp.dot(a_ref[...], wa_ref[:, cols].astype(BF16), preferred_element_type=F32)
        acc += jnp.dot(b_ref[...], wb_ref[:, cols].astype(BF16), preferred_element_type=F32)
        acc += jnp.dot(c_ref[...], wc_ref[:, cols].astype(BF16), preferred_element_type=F32)
        hn = h_ref[:, cols] + acc
        o_ref[:, cols] = hn
        hg_ref[:, cols] = (hn * g_ref[:, cols]).astype(BF16)
        part = _fold_lanes(hn * hn)
        ssq = part if ssq is None else ssq + part

    @pl.when(pl.program_id(1) == 0)
    def _():
        ssq_ref[...] = ssq

    @pl.when(pl.program_id(1) > 0)
    def _():
        ssq_ref[...] += ssq


def _outproj(h, a_o, b_o, c_o, w_out, g, layer):
    t, d = h.shape
    wa, wb, wc = a_o.shape[1], b_o.shape[1], c_o.shape[1]
    assert wa == wb and (wa + wb) % wc == 0
    tm = _pick_tile(t, ROW_TILE, SUBLANES_BF16)
    tn = _pick_tile(d, 512, MXU_COLS)
    return pl.pallas_call(
        _outproj_kernel,
        grid=(t // tm, d // tn),
        in_specs=[
            pl.BlockSpec((tm, wa), lambda i, j: (i, 0)),
            pl.BlockSpec((tm, wb), lambda i, j: (i, 0)),
            pl.BlockSpec((tm, wc), lambda i, j: (i, 0)),
            pl.BlockSpec((None, wa, tn), lambda i, j: (layer, 0, j)),
            pl.BlockSpec((None, wb, tn), lambda i, j: (layer, 1, j)),
            pl.BlockSpec((None, wc, tn), lambda i, j: (layer, (wa + wb) // wc, j)),
            pl.BlockSpec((tm, tn), lambda i, j: (i, j)),
            pl.BlockSpec((None, 1, tn), lambda i, j: (layer, 0, j)),
        ],
        out_specs=[pl.BlockSpec((tm, tn), lambda i, j: (i, j)),
                   pl.BlockSpec((tm, tn), lambda i, j: (i, j)),
                   pl.BlockSpec((tm, LANES), lambda i, j: (i, 0))],
        out_shape=[jax.ShapeDtypeStruct((t, d), F32), jax.ShapeDtypeStruct((t, d), BF16),
                   jax.ShapeDtypeStruct((t, LANES), F32)],
        compiler_params=_params("parallel", "arbitrary"),
        name="outproj_residual",
    )(a_o, b_o, c_o, w_out, w_out, w_out, h, g)


def _down_kernel(a_ref, w_ref, h_ref, o_ref):
    @pl.when(pl.program_id(2) == 0)
    def _():
        o_ref[...] = h_ref[...]
    o_ref[...] += jnp.dot(a_ref[...], w_ref[...].astype(BF16), preferred_element_type=F32)


def _down_residual(h, a, w, layer):
    t, d = h.shape
    f = a.shape[1]
    tm = _pick_tile(t, ROW_TILE, SUBLANES_BF16)
    tn = _pick_tile(d, 2048, LANES)
    tk = _pick_tile(f, 1024, LANES)
    return pl.pallas_call(
        _down_kernel,
        grid=(t // tm, d // tn, f // tk),
        in_specs=[
            pl.BlockSpec((tm, tk), lambda i, j, k: (i, k)),
            pl.BlockSpec((None, tk, tn), lambda i, j, k: (layer, k, j)),
            pl.BlockSpec((tm, tn), lambda i, j, k: (i, j), pipeline_mode=pl.Buffered(1)),
        ],
        out_specs=pl.BlockSpec((tm, tn), lambda i, j, k: (i, j)),
        out_shape=jax.ShapeDtypeStruct((t, d), F32),
        compiler_params=_params("parallel", "parallel", "arbitrary"),
        name="down_residual",
    )(a, w, h)


def _rmsnorm_kernel(x_ref, g_ref, o_ref):
    x = x_ref[...]
    ms = jnp.mean(x * x, axis=-1, keepdims=True)
    o_ref[...] = x * lax.rsqrt(ms + EPS) * g_ref[...]


def _rmsnorm(x, g, *, row0, rows):
    d = x.shape[1]
    tm = _pick_tile(rows, 256, 8)
    assert row0 % tm == 0
    return pl.pallas_call(
        _rmsnorm_kernel,
        grid=(rows // tm,),
        in_specs=[pl.BlockSpec((tm, d), lambda i: (row0 // tm + i, 0)), pl.BlockSpec((1, d), lambda i: (0, 0))],
        out_specs=pl.BlockSpec((tm, d), lambda i: (i, 0)),
        out_shape=jax.ShapeDtypeStruct((rows, d), F32),
        compiler_params=_params("parallel"),
        name="final_rmsnorm",
    )(x, g.reshape(1, d))


def _toeplitz_bias(tbl_ref, base, rows, width):
    m = lax.broadcasted_iota(jnp.int32, (8, width), 1)
    idx = jnp.where(m >= BAND + rows, 2 * REL_CLIP, jnp.clip(BAND + REL_CLIP - m, 0, 2 * REL_CLIP))

    def body(d, g):
        return jnp.where(idx == d, tbl_ref[base + d], g)

    g = lax.fori_loop(0, TABLE, body, jnp.zeros((8, width), F32))
    full = jnp.broadcast_to(g[0:1, :], (rows, width))
    return pltpu.roll(full, 0, 1, stride=1, stride_axis=0)


def _nt_dot(a, b):
    return lax.dot_general(a, b, (((1,), (1,)), ((), ())), preferred_element_type=F32)


def _attn_prompt_kernel(tbl_ref, q_ref, k_ref, v_ref, o_ref, bias_ref, kb_ref, vb_ref, *, layer, heads, tq, nkb):
    head, b = pl.program_id(0), pl.program_id(1)
    seq = q_ref.shape[0]
    tk = nkb * tq

    @pl.when(b == 0)
    def _():
        bias = _toeplitz_bias(tbl_ref, (layer * heads + head) * TABLE, tq, tk + tq)[:, :tk]
        qc = lax.broadcasted_iota(jnp.int32, (tq, tk), 0) // CHUNK
        kc = lax.broadcasted_iota(jnp.int32, (tq, tk), 1) // CHUNK
        bias_ref[...] = jnp.where((kc >= qc) & (kc <= qc + BAND_CHUNKS), bias * LOG2_E, NEG_INF)

    kb_ref[...] = k_ref[...].astype(BF16)
    vb_ref[:, 0:HEAD_DIM] = v_ref[...].astype(BF16)
    vb_ref[:, HEAD_DIM:2 * HEAD_DIM] = jnp.ones((seq, HEAD_DIM), BF16)
    scale = HEAD_DIM ** -0.5 * LOG2_E

    def tile(q0, k0, nk, bias):
        q = (q_ref[pl.ds(q0, tq), :] * scale).astype(BF16)
        s = _nt_dot(q, kb_ref[pl.ds(k0, nk), :]) + bias
        m = jnp.max(s, axis=-1, keepdims=True)
        p = jnp.exp2(s - m)
        ol = jnp.dot(p.astype(BF16), vb_ref[pl.ds(k0, nk), :], preferred_element_type=F32)
        o_ref[pl.ds(q0, tq), :] = (ol[:, 0:HEAD_DIM] / ol[:, HEAD_DIM:2 * HEAD_DIM]).astype(o_ref.dtype)

    for i in range(nkb - 1):
        nk = (i + 1) * tq
        tile(i * tq, 0, nk, bias_ref[:, tk - nk:tk])

    def body(i, carry):
        q0 = pl.multiple_of(i * tq, tq)
        k0 = pl.multiple_of((i - (nkb - 1)) * tq, tq)
        tile(q0, k0, tk, bias_ref[...])
        return carry

    lax.fori_loop(nkb - 1, seq // tq, body, 0, unroll=LOOP_UNROLL)


def _attn_prompt(z, table, layer, *, batch, seq, heads, t_total):
    tq = ATTN_TQ
    assert seq % tq == 0 and BAND % tq == 0 and tq % CHUNK == 0 and seq // tq >= BAND // tq + 1
    nkb = BAND // tq + 1
    width = heads * HEAD_DIM
    spec = lambda c: pl.BlockSpec((seq, HEAD_DIM), lambda h, b: (b, c + h))
    return pl.pallas_call(
        functools.partial(_attn_prompt_kernel, layer=layer, heads=heads, tq=tq, nkb=nkb),
        grid=(heads, batch),
        in_specs=[pl.BlockSpec(memory_space=pltpu.SMEM), spec(0), spec(heads), spec(2 * heads)],
        out_specs=pl.BlockSpec((seq, HEAD_DIM), lambda h, b: (b, h)),
        out_shape=jax.ShapeDtypeStruct((t_total, width), BF16),
        scratch_shapes=[pltpu.VMEM((tq, nkb * tq), F32), pltpu.VMEM((seq, HEAD_DIM), BF16),
                        pltpu.VMEM((seq, 2 * HEAD_DIM), BF16)],
        compiler_params=_params("arbitrary", "arbitrary"),
        name="band_attention_prompt",
    )(table.reshape(-1), z, z, z)


def _attn_sample_kernel(tbl_ref, q_ref, kn_ref, vn_ref, kc_ref, vc_ref, prev_ref, o_ref, bias_ref, *, layer, heads):
    del prev_ref
    s_len = q_ref.shape[0]
    la = kc_ref.shape[1]

    @pl.when(pl.program_id(0) == 0)
    def _():
        for h in range(heads):
            bias_ref[h] = _toeplitz_bias(tbl_ref, (layer * heads + h) * TABLE, s_len, bias_ref.shape[2])

    scale = HEAD_DIM ** -0.5
    for h in range(heads):
        sl = slice(h * HEAD_DIM, (h + 1) * HEAD_DIM)
        q = q_ref[:, sl].astype(BF16)
        sc = _nt_dot(q, kc_ref[h].astype(BF16)) * scale + bias_ref[h, :, 0:la]
        sn = _nt_dot(q, kn_ref[:, sl].astype(BF16)) * scale + bias_ref[h, :, la:la + s_len]
        m = jnp.maximum(jnp.max(sc, axis=-1, keepdims=True), jnp.max(sn, axis=-1, keepdims=True))
        pc, pn = jnp.exp(sc - m), jnp.exp(sn - m)
        l = jnp.sum(pc, axis=-1, keepdims=True) + jnp.sum(pn, axis=-1, keepdims=True)
        o = jnp.dot(pc.astype(BF16), vc_ref[h].astype(BF16), preferred_element_type=F32)
        o += jnp.dot(pn.astype(BF16), vn_ref[:, sl].astype(BF16), preferred_element_type=F32)
        o_ref[:, sl] = (o / l).astype(o_ref.dtype)


def _attn_sample(z, cache_k, cache_v, table, prev, layer, *, row0, batch, s_len, heads):
    la = cache_k.shape[3]
    assert la == BAND, "the relative-position bias tile assumes a full band of cached rows"
    width = heads * HEAD_DIM
    rb0 = row0 // s_len
    bias_w = pl.cdiv(la + 2 * s_len, LANES) * LANES
    new_spec = lambda c: pl.BlockSpec((s_len, width), lambda b: (rb0 + b, c))
    cache_spec = pl.BlockSpec((None, None, heads, la, HEAD_DIM), lambda b: (layer, b, 0, 0, 0))
    return pl.pallas_call(
        functools.partial(_attn_sample_kernel, layer=layer, heads=heads),
        grid=(batch,),
        in_specs=[pl.BlockSpec(memory_space=pltpu.SMEM), new_spec(0), new_spec(1), new_spec(2),
                  cache_spec, cache_spec, pl.BlockSpec(memory_space=pl.ANY)],
        out_specs=pl.BlockSpec((s_len, width), lambda b: (rb0 + b, 0)),
        out_shape=jax.ShapeDtypeStruct(prev.shape, prev.dtype),
        scratch_shapes=[pltpu.VMEM((heads, s_len, bias_w), F32)],
        input_output_aliases={6: 0},
        compiler_params=_params("arbitrary"),
        name="band_attention_sample",
    )(table.reshape(-1), z, z, z, cache_k, cache_v, prev)


def _ret_kernel(lg_ref, gb_ref, q_ref, k_ref, v_ref, gate_ref, cos_ref, sin_ref, gain_ref, s0_ref, *rest,
                cb, nblk, hp, aliased):
    if aliased:
        rest = rest[1:]
    o_ref, sfin_ref = rest[0], rest[1]
    hgroup = pl.program_id(1)
    row = lax.broadcasted_iota(jnp.int32, (cb, cb), 0)
    col = lax.broadcasted_iota(jnp.int32, (cb, cb), 1)
    diff = (row - col).astype(F32)
    t = lax.broadcasted_iota(jnp.int32, (cb, HEAD_DIM), 0).astype(F32)

    for hh in range(hp):
        head = hgroup * hp + hh
        log_g = lg_ref[head]
        g_block = gb_ref[head]
        sl = slice(hh * HEAD_DIM, (hh + 1) * HEAD_DIM)
        decay = jnp.where(diff >= 0, jnp.exp(jnp.maximum(diff, 0.0) * log_g), 0.0)
        if nblk > 1:
            decay_ref = rest[2]
            decay_ref[...] = decay
        q_w = jnp.exp((t + 1.0) * log_g)
        k_w = jnp.exp((cb - 1.0 - t) * log_g)
        gain = gain_ref[:, sl]

        def block(n, state, sl=sl, q_w=q_w, k_w=k_w, gain=gain, decay=decay):
            r = pl.multiple_of(n * cb, cb)
            rows = pl.ds(r, cb)
            cos, sin = cos_ref[rows, :], sin_ref[rows, :]

            def rot(x):
                return x * cos + pltpu.roll(x, HEAD_DIM // 2, 1) * sin

            q = rot(q_ref[rows, sl])
            k = rot(k_ref[rows, sl]) * (HEAD_DIM ** -0.5)
            v = v_ref[rows, sl].astype(BF16)
            qb = q.astype(BF16)
            dec = rest[2][...] if nblk > 1 else decay
            scores = _nt_dot(qb, k.astype(BF16)) * dec
            o = jnp.dot(scores.astype(BF16), v, preferred_element_type=F32)
            o += jnp.dot(qb, state.astype(BF16), preferred_element_type=F32) * q_w
            kv = lax.dot_general((k * k_w).astype(BF16), v, (((0,), (0,)), ((), ())), preferred_element_type=F32)
            mu = jnp.mean(o, axis=-1, keepdims=True)
            var = jnp.mean(jnp.square(o - mu), axis=-1, keepdims=True)
            on = (o - mu) * lax.rsqrt(var + EPS) * gain
            gate = gate_ref[rows, sl]
            o_ref[rows, sl] = (gate * jax.nn.sigmoid(gate) * on).astype(o_ref.dtype)
            return state * g_block + kv

        state = s0_ref[hh]
        if nblk > 1:
            state = lax.fori_loop(0, nblk, block, state, unroll=LOOP_UNROLL)
        else:
            state = block(0, state)
        sfin_ref[hh] = state


def _retention(z, s0, gain, pos, prev, layer, *, row0, batch, seq, cb, hp, heads, col0, t_total):
    nblk = seq // cb
    assert seq % cb == 0 and heads % hp == 0 and row0 % seq == 0 and col0 % hp == 0
    rb0 = row0 // seq
    width = heads * HEAD_DIM
    hidx = jnp.arange(heads, dtype=F32)
    log_g = jnp.log1p(-jnp.exp2(-5.0 - hidx))
    g_block = jnp.exp(cb * log_g)
    half = HEAD_DIM // 2
    inv = ROPE_BASE ** (-jnp.arange(half, dtype=F32) / half)
    ang = pos[:, None] * inv[None, :]
    cos = jnp.concatenate([jnp.cos(ang), jnp.cos(ang)], axis=-1)
    sin = jnp.concatenate([-jnp.sin(ang), jnp.sin(ang)], axis=-1)

    bw = hp * HEAD_DIM
    zspec = lambda c: pl.BlockSpec((seq, bw), lambda b, h: (rb0 + b, (c + h * hp) // hp))
    rope_spec = pl.BlockSpec((seq, HEAD_DIM), lambda b, h: (0, 0))
    smem = pl.BlockSpec(memory_space=pltpu.SMEM)
    if s0.ndim == 5:
        s0_spec = pl.BlockSpec((None, None, hp, HEAD_DIM, HEAD_DIM), lambda b, h: (layer, b, h, 0, 0))
    else:
        s0_spec = pl.BlockSpec((None, hp, HEAD_DIM, HEAD_DIM), lambda b, h: (b, h, 0, 0))
    gain_spec = pl.BlockSpec((None, 1, bw), lambda b, h: (layer, 0, h))
    in_specs = [smem, smem, zspec(col0), zspec(col0 + heads), zspec(col0 + 2 * heads), zspec(col0 + 3 * heads),
                rope_spec, rope_spec, gain_spec, s0_spec]
    args = [log_g, g_block, z, z, z, z, cos, sin, gain, s0]
    aliases = {}
    if prev is not None:
        in_specs.append(pl.BlockSpec(memory_space=pl.ANY))
        args.append(prev)
        aliases = {len(args) - 1: 0}
    return pl.pallas_call(
        functools.partial(_ret_kernel, cb=cb, nblk=nblk, hp=hp, aliased=prev is not None),
        grid=(batch, heads // hp),
        in_specs=in_specs,
        out_specs=[pl.BlockSpec((seq, bw), lambda b, h: (rb0 + b, h)),
                   pl.BlockSpec((None, hp, HEAD_DIM, HEAD_DIM), lambda b, h: (b, h, 0, 0))],
        out_shape=[jax.ShapeDtypeStruct((t_total, width), BF16),
                   jax.ShapeDtypeStruct((batch, heads, HEAD_DIM, HEAD_DIM), F32)],
        scratch_shapes=[pltpu.VMEM((cb, cb), F32)] if nblk > 1 else [],
        input_output_aliases=aliases,
        compiler_params=_params("arbitrary", "arbitrary"),
        name="retention_prompt" if prev is None else "retention_sample",
    )(*args)


def _pool_kernel(u_ref, halo_ref, w_ref, sc_ref, *rest, pos0, nseq, first_tile_has_no_history, aliased):
    if aliased:
        rest = rest[1:]
    o_ref, x_ref = rest
    i, g = pl.program_id(1), pl.program_id(2)
    tr = u_ref.shape[0]
    sr = tr // nseq
    for s in range(nseq):
        halo = halo_ref[s] if nseq > 1 else halo_ref[...]
        if first_tile_has_no_history:
            halo = jnp.where(i == 0, 0.0, halo)
        x_ref[s, 0:POOL_HALO, :] = halo
        x_ref[s, POOL_HALO:POOL_HALO + sr, :] = u_ref[s * sr:(s + 1) * sr, :]
    pos = (pos0 + i * sr + lax.broadcasted_iota(jnp.int32, (sr, 1), 0)).astype(F32)

    for gi, w in enumerate(POOL_WINDOWS):
        @pl.when(g == gi)
        def _(w=w):
            inv_cnt = 1.0 / jnp.minimum(float(w), pos + 1.0)
            parts = []
            for s in range(nseq):
                win = x_ref[s, POOL_HALO:POOL_HALO + sr, :]
                for j in range(1, w):
                    win = win + x_ref[s, POOL_HALO - j:POOL_HALO - j + sr, :]
                parts.append(win * inv_cnt - x_ref[s, POOL_HALO:POOL_HALO + sr, :])
            diff = parts[0] if nseq == 1 else jnp.concatenate(parts, axis=0)
            y = jnp.dot(diff.astype(BF16), w_ref[...].astype(BF16), preferred_element_type=F32)
            o_ref[...] = (y * sc_ref[...]).astype(o_ref.dtype)


def _pool(z, halo, w_pool, scale, prev, layer, *, row0, batch, seq, tr, pos0, col0, t_total):
    groups, gw = w_pool.shape[1], w_pool.shape[2]
    assert groups == len(POOL_WINDOWS) and max(POOL_WINDOWS) - 1 <= POOL_HALO and tr % POOL_HALO == 0
    in_specs = []
    if halo is None:
        nseq, nt, nb = 1, seq // tr, batch
        rb0 = row0 // tr
        per = tr // POOL_HALO
        in_specs += [pl.BlockSpec((tr, gw), lambda b, i, g: (rb0 + b * nt + i, col0 + g)),
                     pl.BlockSpec((POOL_HALO, gw),
                                  lambda b, i, g: (jnp.maximum((rb0 + b * nt + i) * per - 1, 0), col0 + g))]
        args = [z, z]
    else:
        assert tr == batch * seq and row0 % tr == 0
        nseq, nt, nb = batch, 1, 1
        rb0 = row0 // tr
        in_specs += [pl.BlockSpec((tr, gw), lambda b, i, g: (rb0, col0 + g)),
                     pl.BlockSpec((None, batch, POOL_HALO, gw), lambda b, i, g: (layer, 0, 0, g))]
        args = [z, halo]
    in_specs += [pl.BlockSpec((None, None, gw, gw), lambda b, i, g: (layer, g, 0, 0)),
                 pl.BlockSpec((None, 1, gw), lambda b, i, g: (layer, 0, g))]
    args += [w_pool, scale]
    aliases = {}
    if prev is not None:
        in_specs.append(pl.BlockSpec(memory_space=pl.ANY))
        args.append(prev)
        aliases = {len(args) - 1: 0}
    return pl.pallas_call(
        functools.partial(_pool_kernel, pos0=pos0, nseq=nseq, first_tile_has_no_history=halo is None,
                          aliased=prev is not None),
        grid=(nb, nt, groups),
        in_specs=in_specs,
        out_specs=pl.BlockSpec((tr, gw), lambda b, i, g: (rb0 + b * nt + i, g)),
        out_shape=jax.ShapeDtypeStruct((t_total, groups * gw), BF16),
        scratch_shapes=[pltpu.VMEM((nseq, POOL_HALO + tr // nseq, gw), F32)],
        input_output_aliases=aliases,
        compiler_params=_params("arbitrary", "arbitrary", "arbitrary"),
        name="pool_prompt" if prev is None else "pool_sample",
    )(*args)


def kernel(x_prompt, x_sample, cache_band_k, cache_band_v, state_retention, state_pool, norm_mix, w_in,
           rel_bias_table, ret_norm, pool_w, pool_scale, w_out, norm_ffn, w_up, w_down, norm_final):
    bp, lp, d = x_prompt.shape
    bs, ls, _ = x_sample.shape
    depth = w_in.shape[0]
    a_heads = rel_bias_table.shape[1]
    a_width = a_heads * HEAD_DIM
    b_width = ret_norm.shape[1]
    b_heads = b_width // HEAD_DIM
    gw = pool_w.shape[2]
    tp, ts = bp * lp, bs * ls
    t_total = tp + ts
    keep = min(BAND, lp)
    pool_ctx = state_pool.shape[2]
    assert a_width % gw == 0 and (3 * a_width + 4 * b_width) % gw == 0 and tp % ts == 0
    bq_col = 3 * a_heads
    cu_col = (3 * a_width + 4 * b_width) // gw
    cu0 = cu_col * gw

    pos_p = jnp.arange(lp, dtype=F32)
    pos_s = PAST_LEN + jnp.arange(ls, dtype=F32)
    zero_state = jnp.zeros((bp, b_heads, HEAD_DIM, HEAD_DIM), F32)
    pool_halo = jnp.pad(state_pool, ((0, 0), (0, 0), (POOL_HALO - pool_ctx, 0), (0, 0)))
    cache_k = jnp.transpose(cache_band_k, (0, 1, 3, 2, 4))
    cache_v = jnp.transpose(cache_band_v, (0, 1, 3, 2, 4))
    norm_mix, norm_ffn, ret_norm, pool_scale = (
        g.reshape(depth, 1, -1) for g in (norm_mix, norm_ffn, ret_norm, pool_scale))

    def tail(z, b, rows, c0, c1):
        return lax.slice(z, ((b + 1) * lp - rows, c0), ((b + 1) * lp, c1))

    kp_l, vp_l, rp_l, pp_l, ks_l, vs_l, rs_l, ps_l = [], [], [], [], [], [], [], []
    for l in range(depth):
        if l == 0:
            h, hg, ssq = _stage([x_prompt.reshape(tp, d), x_sample.reshape(ts, d)], norm_mix, l)
        else:
            hg, ssq = _stage([h], norm_mix, l)
        z = _scaled_matmul(hg, ssq, w_in, l, relu2=False, out_dtype=F32)

        a_o = _attn_prompt(z, rel_bias_table, l, batch=bp, seq=lp, heads=a_heads, t_total=t_total)
        a_o = _attn_sample(z, cache_k, cache_v, rel_bias_table, a_o, l,
                           row0=tp, batch=bs, s_len=ls, heads=a_heads)
        b_o, s_fin = _retention(z, zero_state, ret_norm, pos_p, None, l, row0=0, batch=bp, seq=lp,
                                cb=RET_BLOCK, hp=1, heads=b_heads, col0=bq_col, t_total=t_total)
        b_o, s_new = _retention(z, state_retention, ret_norm, pos_s, b_o, l, row0=tp, batch=bs, seq=ls,
                                cb=ls, hp=b_heads, heads=b_heads, col0=bq_col, t_total=t_total)
        c_o = _pool(z, None, pool_w, pool_scale, None, l, row0=0, batch=bp, seq=lp,
                    tr=_pick_tile(lp, POOL_ROWS, POOL_HALO), pos0=0, col0=cu_col, t_total=t_total)
        c_o = _pool(z, pool_halo, pool_w, pool_scale, c_o, l, row0=tp, batch=bs, seq=ls,
                    tr=ts, pos0=PAST_LEN, col0=cu_col, t_total=t_total)

        h, hg, ssq = _outproj(h, a_o, b_o, c_o, w_out, norm_ffn, l)
        act = _scaled_matmul(hg, ssq, w_up, l, relu2=True, out_dtype=BF16)
        h = _down_residual(h, act, w_down, l)

        kp_l.append(jnp.stack([tail(z, b, keep, a_width, 2 * a_width) for b in range(bp)])
                    .reshape(bp, keep, a_heads, HEAD_DIM))
        vp_l.append(jnp.stack([tail(z, b, keep, 2 * a_width, 3 * a_width) for b in range(bp)])
                    .reshape(bp, keep, a_heads, HEAD_DIM))
        rp_l.append(s_fin)
        pp_l.append(jnp.stack([tail(z, b, pool_ctx, cu0, z.shape[1]) for b in range(bp)]))
        zs_kv = lax.slice(z, (tp, a_width), (t_total, 3 * a_width)).reshape(bs, ls, 2, a_heads, HEAD_DIM)
        ks_l.append(zs_kv[:, :, 0])
        vs_l.append(zs_kv[:, :, 1])
        rs_l.append(s_new)
        cu_s = lax.slice(z, (tp, cu0), (t_total, z.shape[1])).reshape(bs, ls, -1)
        ps_l.append(jnp.concatenate([state_pool[l], cu_s], axis=1)[:, -pool_ctx:])

    y_prompt = _rmsnorm(h, norm_final, row0=0, rows=tp)
    y_sample = _rmsnorm(h, norm_final, row0=tp, rows=ts)
    return (y_prompt.reshape(bp, lp, d), y_sample.reshape(bs, ls, d),
            jnp.stack(kp_l), jnp.stack(vp_l), jnp.stack(rp_l), jnp.stack(pp_l),
            jnp.stack(ks_l), jnp.stack(vs_l), jnp.stack(rs_l), jnp.stack(ps_l))
```

```python
import functools
import math

import jax
import jax.numpy as jnp
from jax import lax
from jax.experimental import pallas as pl
from jax.experimental.pallas import tpu as pltpu

F32 = jnp.float32
BF16 = jnp.bfloat16

PAST_LEN = 1024
CHUNK = 64
BAND_CHUNKS = 8
BAND = BAND_CHUNKS * CHUNK
HEAD_DIM = 128
REL_CLIP = 128
TABLE = 2 * REL_CLIP + 1
POOL_WINDOWS = (2, 4, 8, 16)
POOL_HALO = 16
ROPE_BASE = 10000.0
EPS = 1e-6
NEG_INF = -1e30
LOG2_E = 1.4426950408889634

LANES = 128
SUBLANES_BF16 = 16
MXU_COLS = 256
VMEM_LIMIT_BYTES = 60 * 1024 * 1024

ATTN_TQ = 256
RET_BLOCK = 256
LOOP_UNROLL = 2
STAGE_ROWS = 256
NORM_ROWS = 64
ROW_TILE = 1100
WIDE_ROW_TILE = 2200
POOL_ROWS = 2048


def _pick_tile(n, target, align):
    best = None
    for t in range(align, min(n, target) + 1, align):
        if n % t == 0:
            best = t
    if best is None:
        raise ValueError(f"no tile for {n} (align {align}, target {target})")
    return best


def _params(*sem):
    return pltpu.CompilerParams(dimension_semantics=sem, vmem_limit_bytes=VMEM_LIMIT_BYTES)


def _fold_lanes(y):
    return functools.reduce(jnp.add, [y[:, c:c + LANES] for c in range(0, y.shape[1], LANES)])


def _stage_kernel(*refs, n_first, copy):
    n_out = 2 if copy else 1
    x_refs, g_ref, outs = refs[:-n_out - 1], refs[-n_out - 1], refs[-n_out:]

    def emit(x_ref):
        def body(c, carry):
            rows = pl.ds(pl.multiple_of(c * NORM_ROWS, NORM_ROWS), NORM_ROWS)
            x = x_ref[rows, :]
            if copy:
                outs[0][rows, :] = x
            ms = jnp.mean(x * x, axis=-1, keepdims=True)
            outs[-1][rows, :] = (x * lax.rsqrt(ms + EPS) * g_ref[...]).astype(BF16)
            return carry
        lax.fori_loop(0, x_ref.shape[0] // NORM_ROWS, body, 0)

    if len(x_refs) == 1:
        emit(x_refs[0])
    else:
        @pl.when(pl.program_id(0) < n_first)
        def _():
            emit(x_refs[0])

        @pl.when(pl.program_id(0) >= n_first)
        def _():
            emit(x_refs[1])


def _stage(xs, g, layer):
    d = xs[0].shape[1]
    tm = _pick_tile(math.gcd(*[x.shape[0] for x in xs]), STAGE_ROWS, NORM_ROWS)
    n_first = xs[0].shape[0] // tm
    t = sum(x.shape[0] for x in xs)
    copy = len(xs) > 1
    in_specs = [pl.BlockSpec((tm, d), lambda i: (jnp.minimum(i, n_first - 1), 0))]
    if copy:
        in_specs.append(pl.BlockSpec((tm, d), lambda i: (jnp.maximum(i - n_first, 0), 0)))
    in_specs.append(pl.BlockSpec((None, 1, d), lambda i: (layer, 0, 0)))
    out_specs = [pl.BlockSpec((tm, d), lambda i: (i, 0))]
    out_shape = [jax.ShapeDtypeStruct((t, d), BF16)]
    if copy:
        out_specs.insert(0, pl.BlockSpec((tm, d), lambda i: (i, 0)))
        out_shape.insert(0, jax.ShapeDtypeStruct((t, d), F32))
    return pl.pallas_call(
        functools.partial(_stage_kernel, n_first=n_first, copy=copy),
        grid=(t // tm,),
        in_specs=in_specs,
        out_specs=out_specs,
        out_shape=out_shape,
        compiler_params=_params("parallel"),
        name="stage_rows",
    )(*xs, g)


def _matmul_kernel(x_ref, w_ref, o_ref):
    for c in range(0, o_ref.shape[1], MXU_COLS):
        y = jnp.dot(x_ref[...], w_ref[:, c:c + MXU_COLS].astype(BF16), preferred_element_type=F32)
        o_ref[:, c:c + MXU_COLS] = y.astype(o_ref.dtype)


def _scaled_matmul_relu2_kernel(hg_ref, ssq_ref, w_ref, o_ref, r_ref):
    @pl.when(pl.program_id(1) == 0)
    def _():
        ms = jnp.sum(ssq_ref[...], axis=-1, keepdims=True) * (1.0 / hg_ref.shape[1])
        r_ref[...] = jnp.broadcast_to(lax.rsqrt(ms + EPS), r_ref.shape)
    for c in range(0, o_ref.shape[1], MXU_COLS):
        y = jnp.dot(hg_ref[...], w_ref[:, c:c + MXU_COLS].astype(BF16), preferred_element_type=F32)
        for cc in range(0, MXU_COLS, LANES):
            yc = jnp.maximum(y[:, cc:cc + LANES] * r_ref[...], 0.0)
            o_ref[:, c + cc:c + cc + LANES] = (yc * yc).astype(o_ref.dtype)


def _wide_matmul(x, w, layer, *, ssq, out_dtype):
    t, k = x.shape
    n = w.shape[2]
    tm = _pick_tile(t, WIDE_ROW_TILE, SUBLANES_BF16)
    tn = _pick_tile(n, 512, MXU_COLS)
    x_spec = pl.BlockSpec((tm, k), lambda i, j: (i, 0), pipeline_mode=pl.Buffered(1))
    w_spec = pl.BlockSpec((None, k, tn), lambda i, j: (layer, 0, j))
    if ssq is None:
        body, in_specs, args, scratch = _matmul_kernel, [x_spec, w_spec], (x, w), []
    else:
        body = _scaled_matmul_relu2_kernel
        in_specs = [x_spec, pl.BlockSpec((tm, LANES), lambda i, j: (i, 0)), w_spec]
        args, scratch = (x, ssq, w), [pltpu.VMEM((tm, LANES), F32)]
    return pl.pallas_call(
        body,
        grid=(t // tm, n // tn),
        in_specs=in_specs,
        out_specs=pl.BlockSpec((tm, tn), lambda i, j: (i, j)),
        out_shape=jax.ShapeDtypeStruct((t, n), out_dtype),
        scratch_shapes=scratch,
        compiler_params=_params("parallel", "arbitrary"),
        name="matmul" if ssq is None else "scaled_matmul_relu2",
    )(*args)


def _outproj_kernel(a_ref, b_ref, c_ref, wa_ref, wb_ref, wc_ref, h_ref, g_ref, o_ref, hg_ref, ssq_ref):
    ssq = None
    for c in range(0, o_ref.shape[1], MXU_COLS):
        cols = slice(c, c + MXU_COLS)
        acc = jnp.dot(a_ref[...], wa_ref[:, cols].astype(BF16), preferred_element_type=F32)
        acc += jnp.dot(b_ref[...], wb_ref[:, cols].astype(BF16), preferred_element_type=F32)
        acc += jnp.dot(c_ref[...], wc_ref[:, cols].astype(BF16), preferred_element_type=F32)
        hn = h_ref[:, cols] + acc
        o_ref[:, cols] = hn
        hg_ref[:, cols] = (hn * g_ref[:, cols]).astype(BF16)
        part = _fold_lanes(hn * hn)
        ssq = part if ssq is None else ssq + part

    @pl.when(pl.program_id(1) == 0)
    def _():
        ssq_ref[...] = ssq

    @pl.when(pl.program_id(1) > 0)
    def _():
        ssq_ref[...] += ssq


def _outproj(h, a_o, b_o, c_o, w_out, g, layer):
    t, d = h.shape
    wa, wb, wc = a_o.shape[1], b_o.shape[1], c_o.shape[1]
    assert wa == wb and (wa + wb) % wc == 0
    tm = _pick_tile(t, ROW_TILE, SUBLANES_BF16)
    tn = _pick_tile(d, 512, MXU_COLS)
    return pl.pallas_call(
        _outproj_kernel,
        grid=(t // tm, d // tn),
        in_specs=[
            pl.BlockSpec((tm, wa), lambda i, j: (i, 0)),
            pl.BlockSpec((tm, wb), lambda i, j: (i, 0)),
            pl.BlockSpec((tm, wc), lambda i, j: (i, 0)),
            pl.BlockSpec((None, wa, tn), lambda i, j: (layer, 0, j)),
            pl.BlockSpec((None, wb, tn), lambda i, j: (layer, 1, j)),
            pl.BlockSpec((None, wc, tn), lambda i, j: (layer, (wa + wb) // wc, j)),
            pl.BlockSpec((tm, tn), lambda i, j: (i, j)),
            pl.BlockSpec((None, 1, tn), lambda i, j: (layer, 0, j)),
        ],
        out_specs=[pl.BlockSpec((tm, tn), lambda i, j: (i, j)),
                   pl.BlockSpec((tm, tn), lambda i, j: (i, j)),
                   pl.BlockSpec((tm, LANES), lambda i, j: (i, 0))],
        out_shape=[jax.ShapeDtypeStruct((t, d), F32), jax.ShapeDtypeStruct((t, d), BF16),
                   jax.ShapeDtypeStruct((t, LANES), F32)],
        compiler_params=_params("parallel", "arbitrary"),
        name="outproj_residual",
    )(a_o, b_o, c_o, w_out, w_out, w_out, h, g)


def _down_kernel(a_ref, w_ref, h_ref, o_ref):
    @pl.when(pl.program_id(2) == 0)
    def _():
        o_ref[...] = h_ref[...]
    o_ref[...] += jnp.dot(a_ref[...], w_ref[...].astype(BF16), preferred_element_type=F32)


def _down_residual(h, a, w, layer):
    t, d = h.shape
    f = a.shape[1]
    tm = _pick_tile(t, WIDE_ROW_TILE, SUBLANES_BF16)
    tn = _pick_tile(d, 1024, LANES)
    tk = _pick_tile(f, 1024, LANES)
    return pl.pallas_call(
        _down_kernel,
        grid=(t // tm, d // tn, f // tk),
        in_specs=[
            pl.BlockSpec((tm, tk), lambda i, j, k: (i, k)),
            pl.BlockSpec((None, tk, tn), lambda i, j, k: (layer, k, j)),
            pl.BlockSpec((tm, tn), lambda i, j, k: (i, j)),
        ],
        out_specs=pl.BlockSpec((tm, tn), lambda i, j, k: (i, j)),
        out_shape=jax.ShapeDtypeStruct((t, d), F32),
        compiler_params=_params("parallel", "parallel", "arbitrary"),
        name="down_residual",
    )(a, w, h)


def _rmsnorm_kernel(x_ref, g_ref, o_ref):
    x = x_ref[...]
    ms = jnp.mean(x * x, axis=-1, keepdims=True)
    o_ref[...] = x * lax.rsqrt(ms + EPS) * g_ref[...]


def _rmsnorm(x, g, *, row0, rows):
    d = x.shape[1]
    tm = _pick_tile(rows, 256, 8)
    assert row0 % tm == 0
    return pl.pallas_call(
        _rmsnorm_kernel,
        grid=(rows // tm,),
        in_specs=[pl.BlockSpec((tm, d), lambda i: (row0 // tm + i, 0)), pl.BlockSpec((1, d), lambda i: (0, 0))],
        out_specs=pl.BlockSpec((tm, d), lambda i: (i, 0)),
        out_shape=jax.ShapeDtypeStruct((rows, d), F32),
        compiler_params=_params("parallel"),
        name="final_rmsnorm",
    )(x, g.reshape(1, d))


def _toeplitz_bias(tbl_ref, base, rows, width):
    m = lax.broadcasted_iota(jnp.int32, (8, width), 1)
    idx = jnp.where(m >= BAND + rows, 2 * REL_CLIP, jnp.clip(BAND + REL_CLIP - m, 0, 2 * REL_CLIP))

    def body(d, g):
        return jnp.where(idx == d, tbl_ref[base + d], g)

    g = lax.fori_loop(0, TABLE, body, jnp.zeros((8, width), F32))
    full = jnp.broadcast_to(g[0:1, :], (rows, width))
    return pltpu.roll(full, 0, 1, stride=1, stride_axis=0)


def _nt_dot(a, b):
    return lax.dot_general(a, b, (((1,), (1,)), ((), ())), preferred_element_type=F32)


def _attn_prompt_kernel(tbl_ref, q_ref, k_ref, v_ref, o_ref, bias_ref, kb_ref, vb_ref, *, layer, heads, tq, nkb):
    head, b = pl.program_id(0), pl.program_id(1)
    seq = q_ref.shape[0]
    tk = nkb * tq

    @pl.when(b == 0)
    def _():
        bias = _toeplitz_bias(tbl_ref, (layer * heads + head) * TABLE, tq, tk + tq)[:, :tk]
        qc = lax.broadcasted_iota(jnp.int32, (tq, tk), 0) // CHUNK
        kc = lax.broadcasted_iota(jnp.int32, (tq, tk), 1) // CHUNK
        bias_ref[...] = jnp.where((kc >= qc) & (kc <= qc + BAND_CHUNKS), bias * LOG2_E, NEG_INF)

    kb_ref[...] = k_ref[...].astype(BF16)
    vb_ref[:, 0:HEAD_DIM] = v_ref[...].astype(BF16)
    vb_ref[:, HEAD_DIM:2 * HEAD_DIM] = jnp.ones((seq, HEAD_DIM), BF16)
    scale = HEAD_DIM ** -0.5 * LOG2_E

    def tile(q0, k0, nk, bias):
        q = (q_ref[pl.ds(q0, tq), :] * scale).astype(BF16)
        s = _nt_dot(q, kb_ref[pl.ds(k0, nk), :]) + bias
        m = jnp.max(s, axis=-1, keepdims=True)
        p = jnp.exp2(s - m)
        ol = jnp.dot(p.astype(BF16), vb_ref[pl.ds(k0, nk), :], preferred_element_type=F32)
        o_ref[pl.ds(q0, tq), :] = (ol[:, 0:HEAD_DIM] / ol[:, HEAD_DIM:2 * HEAD_DIM]).astype(o_ref.dtype)

    for i in range(nkb - 1):
        nk = (i + 1) * tq
        tile(i * tq, 0, nk, bias_ref[:, tk - nk:tk])

    def body(i, carry):
        q0 = pl.multiple_of(i * tq, tq)
        k0 = pl.multiple_of((i - (nkb - 1)) * tq, tq)
        tile(q0, k0, tk, bias_ref[...])
        return carry

    lax.fori_loop(nkb - 1, seq // tq, body, 0, unroll=LOOP_UNROLL)


def _attn_prompt(z, table, layer, *, batch, seq, heads, t_total):
    tq = ATTN_TQ
    assert seq % tq == 0 and BAND % tq == 0 and tq % CHUNK == 0 and seq // tq >= BAND // tq + 1
    nkb = BAND // tq + 1
    width = heads * HEAD_DIM
    spec = lambda c: pl.BlockSpec((seq, HEAD_DIM), lambda h, b: (b, c + h))
    return pl.pallas_call(
        functools.partial(_attn_prompt_kernel, layer=layer, heads=heads, tq=tq, nkb=nkb),
        grid=(heads, batch),
        in_specs=[pl.BlockSpec(memory_space=pltpu.SMEM), spec(0), spec(heads), spec(2 * heads)],
        out_specs=pl.BlockSpec((seq, HEAD_DIM), lambda h, b: (b, h)),
        out_shape=jax.ShapeDtypeStruct((t_total, width), BF16),
        scratch_shapes=[pltpu.VMEM((tq, nkb * tq), F32), pltpu.VMEM((seq, HEAD_DIM), BF16),
                        pltpu.VMEM((seq, 2 * HEAD_DIM), BF16)],
        compiler_params=_params("arbitrary", "arbitrary"),
        name="band_attention_prompt",
    )(table.reshape(-1), z, z, z)


def _attn_sample_kernel(tbl_ref, q_ref, kn_ref, vn_ref, kc_ref, vc_ref, prev_ref, o_ref, bias_ref, *, layer, heads):
    del prev_ref
    s_len = q_ref.shape[0]
    la = kc_ref.shape[1]

    @pl.when(pl.program_id(0) == 0)
    def _():
        for h in range(heads):
            bias_ref[h] = _toeplitz_bias(tbl_ref, (layer * heads + h) * TABLE, s_len, bias_ref.shape[2])

    scale = HEAD_DIM ** -0.5
    hs = range(heads)
    sl = [slice(h * HEAD_DIM, (h + 1) * HEAD_DIM) for h in hs]
    q = [q_ref[:, sl[h]].astype(BF16) for h in hs]
    sc = [_nt_dot(q[h], kc_ref[h].astype(BF16)) * scale + bias_ref[h, :, 0:la] for h in hs]
    sn = [_nt_dot(q[h], kn_ref[:, sl[h]].astype(BF16)) * scale + bias_ref[h, :, la:la + s_len] for h in hs]
    m = [jnp.maximum(jnp.max(sc[h], axis=-1, keepdims=True), jnp.max(sn[h], axis=-1, keepdims=True)) for h in hs]
    pc = [jnp.exp(sc[h] - m[h]) for h in hs]
    pn = [jnp.exp(sn[h] - m[h]) for h in hs]
    l = [jnp.sum(pc[h], axis=-1, keepdims=True) + jnp.sum(pn[h], axis=-1, keepdims=True) for h in hs]
    o = [jnp.dot(pc[h].astype(BF16), vc_ref[h].astype(BF16), preferred_element_type=F32)
         + jnp.dot(pn[h].astype(BF16), vn_ref[:, sl[h]].astype(BF16), preferred_element_type=F32) for h in hs]
    for h in hs:
        o_ref[:, sl[h]] = (o[h] / l[h]).astype(o_ref.dtype)


def _attn_sample(z, cache_k, cache_v, table, prev, layer, *, row0, batch, s_len, heads):
    la = cache_k.shape[3]
    assert la == BAND, "the relative-position bias tile assumes a full band of cached rows"
    width = heads * HEAD_DIM
    rb0 = row0 // s_len
    bias_w = pl.cdiv(la + 2 * s_len, LANES) * LANES
    new_spec = lambda c: pl.BlockSpec((s_len, width), lambda b: (rb0 + b, c))
    cache_spec = pl.BlockSpec((None, None, heads, la, HEAD_DIM), lambda b: (layer, b, 0, 0, 0))
    return pl.pallas_call(
        functools.partial(_attn_sample_kernel, layer=layer, heads=heads),
        grid=(batch,),
        in_specs=[pl.BlockSpec(memory_space=pltpu.SMEM), new_spec(0), new_spec(1), new_spec(2),
                  cache_spec, cache_spec, pl.BlockSpec(memory_space=pl.ANY)],
        out_specs=pl.BlockSpec((s_len, width), lambda b: (rb0 + b, 0)),
        out_shape=jax.ShapeDtypeStruct(prev.shape, prev.dtype),
        scratch_shapes=[pltpu.VMEM((heads, s_len, bias_w), F32)],
        input_output_aliases={6: 0},
        compiler_params=_params("arbitrary"),
        name="band_attention_sample",
    )(table.reshape(-1), z, z, z, cache_k, cache_v, prev)


def _ret_kernel(lg_ref, gb_ref, q_ref, k_ref, v_ref, gate_ref, cos_ref, sin_ref, gain_ref, s0_ref, *rest,
                cb, nblk, hp, aliased):
    if aliased:
        rest = rest[1:]
    o_ref, sfin_ref = rest[0], rest[1]
    hgroup = pl.program_id(1)
    row = lax.broadcasted_iota(jnp.int32, (cb, cb), 0)
    col = lax.broadcasted_iota(jnp.int32, (cb, cb), 1)
    diff = (row - col).astype(F32)
    t = lax.broadcasted_iota(jnp.int32, (cb, HEAD_DIM), 0).astype(F32)

    for hh in range(hp):
        head = hgroup * hp + hh
        log_g = lg_ref[head]
        g_block = gb_ref[head]
        sl = slice(hh * HEAD_DIM, (hh + 1) * HEAD_DIM)
        key_scale = HEAD_DIM ** -0.5
        decay = jnp.where(diff >= 0, jnp.exp(jnp.maximum(diff, 0.0) * log_g) * key_scale, 0.0)
        if nblk > 1:
            decay_ref = rest[2]
            decay_ref[...] = decay
        q_w = jnp.exp((t + 1.0) * log_g)
        k_w = jnp.exp((cb - 1.0 - t) * log_g) * key_scale
        gain = gain_ref[:, sl]

        def block(n, state, sl=sl, q_w=q_w, k_w=k_w, gain=gain, decay=decay):
            r = pl.multiple_of(n * cb, cb)
            rows = pl.ds(r, cb)
            cos, sin = cos_ref[rows, :], sin_ref[rows, :]

            def rot(x):
                return x * cos + pltpu.roll(x, HEAD_DIM // 2, 1) * sin

            q = rot(q_ref[rows, sl])
            k = rot(k_ref[rows, sl])
            v = v_ref[rows, sl].astype(BF16)
            qb = q.astype(BF16)
            dec = rest[2][...] if nblk > 1 else decay
            scores = _nt_dot(qb, k.astype(BF16)) * dec
            o = jnp.dot(scores.astype(BF16), v, preferred_element_type=F32)
            o += jnp.dot(qb, state.astype(BF16), preferred_element_type=F32) * q_w
            kv = lax.dot_general((k * k_w).astype(BF16), v, (((0,), (0,)), ((), ())), preferred_element_type=F32)
            mu = jnp.mean(o, axis=-1, keepdims=True)
            var = jnp.mean(jnp.square(o - mu), axis=-1, keepdims=True)
            on = (o - mu) * lax.rsqrt(var + EPS) * gain
            gate = gate_ref[rows, sl]
            o_ref[rows, sl] = (gate * jax.nn.sigmoid(gate) * on).astype(o_ref.dtype)
            return state * g_block + kv

        state = s0_ref[hh]
        if nblk > 1:
            state = lax.fori_loop(0, nblk, block, state, unroll=LOOP_UNROLL)
        else:
            state = block(0, state)
        sfin_ref[hh] = state


def _retention(z, s0, gain, pos, prev, layer, *, row0, batch, seq, cb, hp, heads, col0, t_total):
    nblk = seq // cb
    assert seq % cb == 0 and heads % hp == 0 and row0 % seq == 0 and col0 % hp == 0
    rb0 = row0 // seq
    width = heads * HEAD_DIM
    hidx = jnp.arange(heads, dtype=F32)
    log_g = jnp.log1p(-jnp.exp2(-5.0 - hidx))
    g_block = jnp.exp(cb * log_g)
    half = HEAD_DIM // 2
    inv = ROPE_BASE ** (-jnp.arange(half, dtype=F32) / half)
    ang = pos[:, None] * inv[None, :]
    cos = jnp.concatenate([jnp.cos(ang), jnp.cos(ang)], axis=-1)
    sin = jnp.concatenate([-jnp.sin(ang), jnp.sin(ang)], axis=-1)

    bw = hp * HEAD_DIM
    zspec = lambda c: pl.BlockSpec((seq, bw), lambda b, h: (rb0 + b, (c + h * hp) // hp))
    rope_spec = pl.BlockSpec((seq, HEAD_DIM), lambda b, h: (0, 0))
    smem = pl.BlockSpec(memory_space=pltpu.SMEM)
    if s0.ndim == 5:
        s0_spec = pl.BlockSpec((None, None, hp, HEAD_DIM, HEAD_DIM), lambda b, h: (layer, b, h, 0, 0))
    else:
        s0_spec = pl.BlockSpec((None, hp, HEAD_DIM, HEAD_DIM), lambda b, h: (b, h, 0, 0))
    gain_spec = pl.BlockSpec((None, 1, bw), lambda b, h: (layer, 0, h))
    in_specs = [smem, smem, zspec(col0), zspec(col0 + heads), zspec(col0 + 2 * heads), zspec(col0 + 3 * heads),
                rope_spec, rope_spec, gain_spec, s0_spec]
    args = [log_g, g_block, z, z, z, z, cos, sin, gain, s0]
    aliases = {}
    if prev is not None:
        in_specs.append(pl.BlockSpec(memory_space=pl.ANY))
        args.append(prev)
        aliases = {len(args) - 1: 0}
    return pl.pallas_call(
        functools.partial(_ret_kernel, cb=cb, nblk=nblk, hp=hp, aliased=prev is not None),
        grid=(batch, heads // hp),
        in_specs=in_specs,
        out_specs=[pl.BlockSpec((seq, bw), lambda b, h: (rb0 + b, h)),
                   pl.BlockSpec((None, hp, HEAD_DIM, HEAD_DIM), lambda b, h: (b, h, 0, 0))],
        out_shape=[jax.ShapeDtypeStruct((t_total, width), BF16),
                   jax.ShapeDtypeStruct((batch, heads, HEAD_DIM, HEAD_DIM), F32)],
        scratch_shapes=[pltpu.VMEM((cb, cb), F32)] if nblk > 1 else [],
        input_output_aliases=aliases,
        compiler_params=_params("arbitrary", "arbitrary"),
        name="retention_prompt" if prev is None else "retention_sample",
    )(*args)


def _pool_kernel(u_ref, halo_ref, w_ref, sc_ref, *rest, pos0, nseq, first_tile_has_no_history, aliased):
    if aliased:
        rest = rest[1:]
    o_ref, x_ref = rest
    i, g = pl.program_id(1), pl.program_id(2)
    tr = u_ref.shape[0]
    sr = tr // nseq
    for s in range(nseq):
        halo = halo_ref[s] if nseq > 1 else halo_ref[...]
        if first_tile_has_no_history:
            halo = jnp.where(i == 0, 0.0, halo)
        x_ref[s, 0:POOL_HALO, :] = halo
        x_ref[s, POOL_HALO:POOL_HALO + sr, :] = u_ref[s * sr:(s + 1) * sr, :]
    pos = (pos0 + i * sr + lax.broadcasted_iota(jnp.int32, (sr, 1), 0)).astype(F32)

    for gi, w in enumerate(POOL_WINDOWS):
        @pl.when(g == gi)
        def _(w=w):
            inv_cnt = 1.0 / jnp.minimum(float(w), pos + 1.0)
            parts = []
            for s in range(nseq):
                win = x_ref[s, POOL_HALO:POOL_HALO + sr, :]
                for j in range(1, w):
                    win = win + x_ref[s, POOL_HALO - j:POOL_HALO - j + sr, :]
                parts.append(win * inv_cnt - x_ref[s, POOL_HALO:POOL_HALO + sr, :])
            diff = parts[0] if nseq == 1 else jnp.concatenate(parts, axis=0)
            y = jnp.dot(diff.astype(BF16), w_ref[...].astype(BF16), preferred_element_type=F32)
            o_ref[...] = (y * sc_ref[...]).astype(o_ref.dtype)


def _pool(z, halo, w_pool, scale, prev, layer, *, row0, batch, seq, tr, pos0, col0, t_total):
    groups, gw = w_pool.shape[1], w_pool.shape[2]
    assert groups == len(POOL_WINDOWS) and max(POOL_WINDOWS) - 1 <= POOL_HALO and tr % POOL_HALO == 0
    in_specs = []
    if halo is None:
        nseq, nt, nb = 1, seq // tr, batch
        rb0 = row0 // tr
        per = tr // POOL_HALO
        in_specs += [pl.BlockSpec((tr, gw), lambda b, i, g: (rb0 + b * nt + i, col0 + g)),
                     pl.BlockSpec((POOL_HALO, gw),
                                  lambda b, i, g: (jnp.maximum((rb0 + b * nt + i) * per - 1, 0), col0 + g))]
        args = [z, z]
    else:
        assert tr == batch * seq and row0 % tr == 0
        nseq, nt, nb = batch, 1, 1
        rb0 = row0 // tr
        in_specs += [pl.BlockSpec((tr, gw), lambda b, i, g: (rb0, col0 + g)),
                     pl.BlockSpec((None, batch, POOL_HALO, gw), lambda b, i, g: (layer, 0, 0, g))]
        args = [z, halo]
    in_specs += [pl.BlockSpec((None, None, gw, gw), lambda b, i, g: (layer, g, 0, 0)),
                 pl.BlockSpec((None, 1, gw), lambda b, i, g: (layer, 0, g))]
    args += [w_pool, scale]
    aliases = {}
    if prev is not None:
        in_specs.append(pl.BlockSpec(memory_space=pl.ANY))
        args.append(prev)
        aliases = {len(args) - 1: 0}
    return pl.pallas_call(
        functools.partial(_pool_kernel, pos0=pos0, nseq=nseq, first_tile_has_no_history=halo is None,
                          aliased=prev is not None),
        grid=(nb, nt, groups),
        in_specs=in_specs,
        out_specs=pl.BlockSpec((tr, gw), lambda b, i, g: (rb0 + b * nt + i, g)),
        out_shape=jax.ShapeDtypeStruct((t_total, groups * gw), BF16),
        scratch_shapes=[pltpu.VMEM((nseq, POOL_HALO + tr // nseq, gw), F32)],
        input_output_aliases=aliases,
        compiler_params=_params("arbitrary", "arbitrary", "arbitrary"),
        name="pool_prompt" if prev is None else "pool_sample",
    )(*args)


def kernel(x_prompt, x_sample, cache_band_k, cache_band_v, state_retention, state_pool, norm_mix, w_in,
           rel_bias_table, ret_norm, pool_w, pool_scale, w_out, norm_ffn, w_up, w_down, norm_final):
    bp, lp, d = x_prompt.shape
    bs, ls, _ = x_sample.shape
    depth = w_in.shape[0]
    a_heads = rel_bias_table.shape[1]
    a_width = a_heads * HEAD_DIM
    b_width = ret_norm.shape[1]
    b_heads = b_width // HEAD_DIM
    gw = pool_w.shape[2]
    tp, ts = bp * lp, bs * ls
    t_total = tp + ts
    keep = min(BAND, lp)
    pool_ctx = state_pool.shape[2]
    assert a_width % gw == 0 and (3 * a_width + 4 * b_width) % gw == 0 and tp % ts == 0
    bq_col = 3 * a_heads
    cu_col = (3 * a_width + 4 * b_width) // gw
    cu0 = cu_col * gw

    pos_p = jnp.arange(lp, dtype=F32)
    pos_s = PAST_LEN + jnp.arange(ls, dtype=F32)
    zero_state = jnp.zeros((bp, b_heads, HEAD_DIM, HEAD_DIM), F32)
    pool_halo = jnp.pad(state_pool, ((0, 0), (0, 0), (POOL_HALO - pool_ctx, 0), (0, 0)))
    cache_k = jnp.transpose(cache_band_k, (0, 1, 3, 2, 4))
    cache_v = jnp.transpose(cache_band_v, (0, 1, 3, 2, 4))
    norm_mix, norm_ffn, ret_norm, pool_scale = (
        g.reshape(depth, 1, -1) for g in (norm_mix, norm_ffn, ret_norm, pool_scale))

    def tail(z, b, rows, c0, c1):
        return lax.slice(z, ((b + 1) * lp - rows, c0), ((b + 1) * lp, c1))

    kp_l, vp_l, rp_l, pp_l, ks_l, vs_l, rs_l, ps_l = [], [], [], [], [], [], [], []
    for l in range(depth):
        if l == 0:
            h, xn = _stage([x_prompt.reshape(tp, d), x_sample.reshape(ts, d)], norm_mix, l)
        else:
            xn, = _stage([h], norm_mix, l)
        z = _wide_matmul(xn, w_in, l, ssq=None, out_dtype=F32)

        a_o = _attn_prompt(z, rel_bias_table, l, batch=bp, seq=lp, heads=a_heads, t_total=t_total)
        a_o = _attn_sample(z, cache_k, cache_v, rel_bias_table, a_o, l,
                           row0=tp, batch=bs, s_len=ls, heads=a_heads)
        b_o, s_fin = _retention(z, zero_state, ret_norm, pos_p, None, l, row0=0, batch=bp, seq=lp,
                                cb=RET_BLOCK, hp=1, heads=b_heads, col0=bq_col, t_total=t_total)
        b_o, s_new = _retention(z, state_retention, ret_norm, pos_s, b_o, l, row0=tp, batch=bs, seq=ls,
                                cb=ls, hp=b_heads, heads=b_heads, col0=bq_col, t_total=t_total)
        c_o = _pool(z, None, pool_w, pool_scale, None, l, row0=0, batch=bp, seq=lp,
                    tr=_pick_tile(lp, POOL_ROWS, POOL_HALO), pos0=0, col0=cu_col, t_total=t_total)
        c_o = _pool(z, pool_halo, pool_w, pool_scale, c_o, l, row0=tp, batch=bs, seq=ls,
                    tr=ts, pos0=PAST_LEN, col0=cu_col, t_total=t_total)

        h, hg, ssq = _outproj(h, a_o, b_o, c_o, w_out, norm_ffn, l)
        act = _wide_matmul(hg, w_up, l, ssq=ssq, out_dtype=BF16)
        h = _down_residual(h, act, w_down, l)

        kp_l.append(jnp.stack([tail(z, b, keep, a_width, 2 * a_width) for b in range(bp)])
                    .reshape(bp, keep, a_heads, HEAD_DIM))
        vp_l.append(jnp.stack([tail(z, b, keep, 2 * a_width, 3 * a_width) for b in range(bp)])
                    .reshape(bp, keep, a_heads, HEAD_DIM))
        rp_l.append(s_fin)
        pp_l.append(jnp.stack([tail(z, b, pool_ctx, cu0, z.shape[1]) for b in range(bp)]))
        zs_kv = lax.slice(z, (tp, a_width), (t_total, 3 * a_width)).reshape(bs, ls, 2, a_heads, HEAD_DIM)
        ks_l.append(zs_kv[:, :, 0])
        vs_l.append(zs_kv[:, :, 1])
        rs_l.append(s_new)
        cu_s = lax.slice(z, (tp, cu0), (t_total, z.shape[1])).reshape(bs, ls, -1)
        ps_l.append(jnp.concatenate([state_pool[l], cu_s], axis=1)[:, -pool_ctx:])

    y_prompt = _rmsnorm(h, norm_final, row0=0, rows=tp)
    y_sample = _rmsnorm(h, norm_final, row0=tp, rows=ts)
    return (y_prompt.reshape(bp, lp, d), y_sample.reshape(bs, ls, d),
            jnp.stack(kp_l), jnp.stack(vp_l), jnp.stack(rp_l), jnp.stack(pp_l),
            jnp.stack(ks_l), jnp.stack(vs_l), jnp.stack(rs_l), jnp.stack(ps_l))
```

```python
import functools
import math

import jax
import jax.numpy as jnp
from jax import lax
from jax.experimental import pallas as pl
from jax.experimental.pallas import tpu as pltpu

F32 = jnp.float32
BF16 = jnp.bfloat16

PAST_LEN = 1024
CHUNK = 64
BAND_CHUNKS = 8
BAND = BAND_CHUNKS * CHUNK
HEAD_DIM = 128
REL_CLIP = 128
TABLE = 2 * REL_CLIP + 1
POOL_WINDOWS = (2, 4, 8, 16)
POOL_HALO = 16
ROPE_BASE = 10000.0
EPS = 1e-6
NEG_INF = -1e30
LOG2_E = 1.4426950408889634

LANES = 128
SUBLANES_BF16 = 16
MXU_COLS = 256
VMEM_LIMIT_BYTES = 60 * 1024 * 1024

ATTN_TQ = 256
RET_BLOCK = 256
LOOP_UNROLL = 4
STAGE_ROWS = 256
NORM_ROWS = 64
ROW_TILE = 1100
WIDE_ROW_TILE = 2200
POOL_ROWS = 2048


def _pick_tile(n, target, align):
    best = None
    for t in range(align, min(n, target) + 1, align):
        if n % t == 0:
            best = t
    if best is None:
        raise ValueError(f"no tile for {n} (align {align}, target {target})")
    return best


def _params(*sem):
    return pltpu.CompilerParams(dimension_semantics=sem, vmem_limit_bytes=VMEM_LIMIT_BYTES)


def _fold_lanes(y):
    return functools.reduce(jnp.add, [y[:, c:c + LANES] for c in range(0, y.shape[1], LANES)])


def _stage_kernel(*refs, n_first, copy):
    n_out = 2 if copy else 1
    x_refs, g_ref, outs = refs[:-n_out - 1], refs[-n_out - 1], refs[-n_out:]

    def emit(x_ref):
        def body(c, carry):
            rows = pl.ds(pl.multiple_of(c * NORM_ROWS, NORM_ROWS), NORM_ROWS)
            x = x_ref[rows, :]
            if copy:
                outs[0][rows, :] = x
            ms = jnp.mean(x * x, axis=-1, keepdims=True)
            outs[-1][rows, :] = (x * lax.rsqrt(ms + EPS) * g_ref[...]).astype(BF16)
            return carry
        lax.fori_loop(0, x_ref.shape[0] // NORM_ROWS, body, 0)

    if len(x_refs) == 1:
        emit(x_refs[0])
    else:
        @pl.when(pl.program_id(0) < n_first)
        def _():
            emit(x_refs[0])

        @pl.when(pl.program_id(0) >= n_first)
        def _():
            emit(x_refs[1])


def _stage(xs, g, layer):
    d = xs[0].shape[1]
    tm = _pick_tile(math.gcd(*[x.shape[0] for x in xs]), STAGE_ROWS, NORM_ROWS)
    n_first = xs[0].shape[0] // tm
    t = sum(x.shape[0] for x in xs)
    copy = len(xs) > 1
    in_specs = [pl.BlockSpec((tm, d), lambda i: (jnp.minimum(i, n_first - 1), 0))]
    if copy:
        in_specs.append(pl.BlockSpec((tm, d), lambda i: (jnp.maximum(i - n_first, 0), 0)))
    in_specs.append(pl.BlockSpec((None, 1, d), lambda i: (layer, 0, 0)))
    out_specs = [pl.BlockSpec((tm, d), lambda i: (i, 0))]
    out_shape = [jax.ShapeDtypeStruct((t, d), BF16)]
    if copy:
        out_specs.insert(0, pl.BlockSpec((tm, d), lambda i: (i, 0)))
        out_shape.insert(0, jax.ShapeDtypeStruct((t, d), F32))
    return pl.pallas_call(
        functools.partial(_stage_kernel, n_first=n_first, copy=copy),
        grid=(t // tm,),
        in_specs=in_specs,
        out_specs=out_specs,
        out_shape=out_shape,
        compiler_params=_params("parallel"),
        name="stage_rows",
    )(*xs, g)


def _matmul_kernel(x_ref, w_ref, o_ref, *, relu2):
    for c in range(0, o_ref.shape[1], MXU_COLS):
        y = jnp.dot(x_ref[...], w_ref[:, c:c + MXU_COLS].astype(BF16), preferred_element_type=F32)
        if relu2:
            y = jnp.square(jnp.maximum(y, 0.0))
        o_ref[:, c:c + MXU_COLS] = y.astype(o_ref.dtype)


def _wide_matmul(x, w, layer, *, relu2, out_dtype):
    t, k = x.shape
    n = w.shape[2]
    tm = _pick_tile(t, WIDE_ROW_TILE, SUBLANES_BF16)
    tn = _pick_tile(n, 512, MXU_COLS)
    return pl.pallas_call(
        functools.partial(_matmul_kernel, relu2=relu2),
        grid=(t // tm, n // tn),
        in_specs=[pl.BlockSpec((tm, k), lambda i, j: (i, 0), pipeline_mode=pl.Buffered(1)),
                  pl.BlockSpec((None, k, tn), lambda i, j: (layer, 0, j))],
        out_specs=pl.BlockSpec((tm, tn), lambda i, j: (i, j)),
        out_shape=jax.ShapeDtypeStruct((t, n), out_dtype),
        compiler_params=_params("parallel", "parallel"),
        name="matmul_relu2" if relu2 else "matmul",
    )(x, w)


def _outproj_kernel(a_ref, b_ref, c_ref, wa_ref, wb_ref, wc_ref, h_ref, g_ref, o_ref, hg_ref, ssq_ref):
    ssq = None
    for c in range(0, o_ref.shape[1], MXU_COLS):
        cols = slice(c, c + MXU_COLS)
        acc = jnp.dot(a_ref[...], wa_ref[:, cols].astype(BF16), preferred_element_type=F32)
        acc += jnp.dot(b_ref[...], wb_ref[:, cols].astype(BF16), preferred_element_type=F32)
        acc += jnp.dot(c_ref[...], wc_ref[:, cols].astype(BF16), preferred_element_type=F32)
        hn = h_ref[:, cols] + acc
        o_ref[:, cols] = hn
        hg_ref[:, cols] = (hn * g_ref[:, cols]).astype(BF16)
        part = _fold_lanes(hn * hn)
        ssq = part if ssq is None else ssq + part

    @pl.when(pl.program_id(1) == 0)
    def _():
        ssq_ref[...] = ssq

    @pl.when(pl.program_id(1) > 0)
    def _():
        ssq_ref[...] += ssq


def _outproj(h, a_o, b_o, c_o, w_out, g, layer):
    t, d = h.shape
    wa, wb, wc = a_o.shape[1], b_o.shape[1], c_o.shape[1]
    assert wa == wb and (wa + wb) % wc == 0
    tm = _pick_tile(t, ROW_TILE, SUBLANES_BF16)
    tn = _pick_tile(d, 512, MXU_COLS)
    return pl.pallas_call(
        _outproj_kernel,
        grid=(t // tm, d // tn),
        in_specs=[
            pl.BlockSpec((tm, wa), lambda i, j: (i, 0)),
            pl.BlockSpec((tm, wb), lambda i, j: (i, 0)),
            pl.BlockSpec((tm, wc), lambda i, j: (i, 0)),
            pl.BlockSpec((None, wa, tn), lambda i, j: (layer, 0, j)),
            pl.BlockSpec((None, wb, tn), lambda i, j: (layer, 1, j)),
            pl.BlockSpec((None, wc, tn), lambda i, j: (layer, (wa + wb) // wc, j)),
            pl.BlockSpec((tm, tn), lambda i, j: (i, j)),
            pl.BlockSpec((None, 1, tn), lambda i, j: (layer, 0, j)),
        ],
        out_specs=[pl.BlockSpec((tm, tn), lambda i, j: (i, j)),
                   pl.BlockSpec((tm, tn), lambda i, j: (i, j)),
                   pl.BlockSpec((tm, LANES), lambda i, j: (i, 0))],
        out_shape=[jax.ShapeDtypeStruct((t, d), F32), jax.ShapeDtypeStruct((t, d), BF16),
                   jax.ShapeDtypeStruct((t, LANES), F32)],
        compiler_params=_params("parallel", "arbitrary"),
        name="outproj_residual",
    )(a_o, b_o, c_o, w_out, w_out, w_out, h, g)


def _down_kernel(a_ref, w_ref, h_ref, ssq_ref, o_ref, *, nk, norm_dim):
    k = pl.program_id(2)

    @pl.when(k == 0)
    def _():
        o_ref[...] = jnp.zeros(o_ref.shape, o_ref.dtype)

    o_ref[...] += jnp.dot(a_ref[...], w_ref[...].astype(BF16), preferred_element_type=F32)

    @pl.when(k == nk - 1)
    def _():
        def body(c, carry):
            rows = pl.ds(pl.multiple_of(c * NORM_ROWS, NORM_ROWS), NORM_ROWS)
            ms = jnp.sum(ssq_ref[rows, :], axis=-1, keepdims=True) * (1.0 / norm_dim)
            o_ref[rows, :] = h_ref[rows, :] + o_ref[rows, :] * (1.0 / (ms + EPS))
            return carry
        lax.fori_loop(0, o_ref.shape[0] // NORM_ROWS, body, 0)


def _down_residual(h, ssq, a, w, layer):
    t, d = h.shape
    f = a.shape[1]
    tm = _pick_tile(t, WIDE_ROW_TILE, NORM_ROWS)
    tn = _pick_tile(d, 1024, LANES)
    tk = _pick_tile(f, 1024, LANES)
    return pl.pallas_call(
        functools.partial(_down_kernel, nk=f // tk, norm_dim=d),
        grid=(t // tm, d // tn, f // tk),
        in_specs=[
            pl.BlockSpec((tm, tk), lambda i, j, k: (i, k)),
            pl.BlockSpec((None, tk, tn), lambda i, j, k: (layer, k, j)),
            pl.BlockSpec((tm, tn), lambda i, j, k: (i, j)),
            pl.BlockSpec((tm, LANES), lambda i, j, k: (i, 0), pipeline_mode=pl.Buffered(1)),
        ],
        out_specs=pl.BlockSpec((tm, tn), lambda i, j, k: (i, j)),
        out_shape=jax.ShapeDtypeStruct((t, d), F32),
        compiler_params=_params("parallel", "parallel", "arbitrary"),
        name="down_residual",
    )(a, w, h, ssq)


def _rmsnorm_kernel(x_ref, g_ref, o_ref):
    x = x_ref[...]
    ms = jnp.mean(x * x, axis=-1, keepdims=True)
    o_ref[...] = x * lax.rsqrt(ms + EPS) * g_ref[...]


def _rmsnorm(x, g, *, row0, rows):
    d = x.shape[1]
    tm = _pick_tile(rows, 256, 8)
    assert row0 % tm == 0
    return pl.pallas_call(
        _rmsnorm_kernel,
        grid=(rows // tm,),
        in_specs=[pl.BlockSpec((tm, d), lambda i: (row0 // tm + i, 0)), pl.BlockSpec((1, d), lambda i: (0, 0))],
        out_specs=pl.BlockSpec((tm, d), lambda i: (i, 0)),
        out_shape=jax.ShapeDtypeStruct((rows, d), F32),
        compiler_params=_params("parallel"),
        name="final_rmsnorm",
    )(x, g.reshape(1, d))


def _toeplitz_bias(tbl_ref, base, rows, width):
    m = lax.broadcasted_iota(jnp.int32, (8, width), 1)
    idx = jnp.where(m >= BAND + rows, 2 * REL_CLIP, jnp.clip(BAND + REL_CLIP - m, 0, 2 * REL_CLIP))

    def body(d, g):
        return jnp.where(idx == d, tbl_ref[base + d], g)

    g = lax.fori_loop(0, TABLE, body, jnp.zeros((8, width), F32))
    full = jnp.broadcast_to(g[0:1, :], (rows, width))
    return pltpu.roll(full, 0, 1, stride=1, stride_axis=0)


def _nt_dot(a, b):
    return lax.dot_general(a, b, (((1,), (1,)), ((), ())), preferred_element_type=F32)


def _attn_prompt_kernel(tbl_ref, q_ref, k_ref, v_ref, o_ref, bias_ref, kb_ref, vb_ref, *, layer, heads, tq, nkb):
    head, b = pl.program_id(0), pl.program_id(1)
    seq = q_ref.shape[0]
    tk = nkb * tq

    @pl.when(b == 0)
    def _():
        bias = _toeplitz_bias(tbl_ref, (layer * heads + head) * TABLE, tq, tk + tq)[:, :tk]
        qc = lax.broadcasted_iota(jnp.int32, (tq, tk), 0) // CHUNK
        kc = lax.broadcasted_iota(jnp.int32, (tq, tk), 1) // CHUNK
        bias_ref[...] = jnp.where((kc >= qc) & (kc <= qc + BAND_CHUNKS), bias * LOG2_E, NEG_INF)

    kb_ref[...] = k_ref[...].astype(BF16)
    vb_ref[:, 0:HEAD_DIM] = v_ref[...].astype(BF16)
    vb_ref[:, HEAD_DIM:2 * HEAD_DIM] = jnp.ones((seq, HEAD_DIM), BF16)
    scale = HEAD_DIM ** -0.5 * LOG2_E

    def tile(q0, k0, nk, bias):
        q = (q_ref[pl.ds(q0, tq), :] * scale).astype(BF16)
        s = _nt_dot(q, kb_ref[pl.ds(k0, nk), :]) + bias
        m = jnp.max(s, axis=-1, keepdims=True)
        p = jnp.exp2(s - m)
        ol = jnp.dot(p.astype(BF16), vb_ref[pl.ds(k0, nk), :], preferred_element_type=F32)
        o_ref[pl.ds(q0, tq), :] = (ol[:, 0:HEAD_DIM] / ol[:, HEAD_DIM:2 * HEAD_DIM]).astype(o_ref.dtype)

    for i in range(nkb - 1):
        nk = (i + 1) * tq
        tile(i * tq, 0, nk, bias_ref[:, tk - nk:tk])

    def body(i, carry):
        q0 = pl.multiple_of(i * tq, tq)
        k0 = pl.multiple_of((i - (nkb - 1)) * tq, tq)
        tile(q0, k0, tk, bias_ref[...])
        return carry

    lax.fori_loop(nkb - 1, seq // tq, body, 0, unroll=LOOP_UNROLL)


def _attn_prompt(z, table, layer, *, batch, seq, heads, t_total):
    tq = ATTN_TQ
    assert seq % tq == 0 and BAND % tq == 0 and tq % CHUNK == 0 and seq // tq >= BAND // tq + 1
    nkb = BAND // tq + 1
    width = heads * HEAD_DIM
    spec = lambda c: pl.BlockSpec((seq, HEAD_DIM), lambda h, b: (b, c + h))
    return pl.pallas_call(
        functools.partial(_attn_prompt_kernel, layer=layer, heads=heads, tq=tq, nkb=nkb),
        grid=(heads, batch),
        in_specs=[pl.BlockSpec(memory_space=pltpu.SMEM), spec(0), spec(heads), spec(2 * heads)],
        out_specs=pl.BlockSpec((seq, HEAD_DIM), lambda h, b: (b, h)),
        out_shape=jax.ShapeDtypeStruct((t_total, width), BF16),
        scratch_shapes=[pltpu.VMEM((tq, nkb * tq), F32), pltpu.VMEM((seq, HEAD_DIM), BF16),
                        pltpu.VMEM((seq, 2 * HEAD_DIM), BF16)],
        compiler_params=_params("arbitrary", "arbitrary"),
        name="band_attention_prompt",
    )(table.reshape(-1), z, z, z)


def _attn_sample_kernel(tbl_ref, q_ref, kn_ref, vn_ref, kc_ref, vc_ref, prev_ref, o_ref, bias_ref, *, layer, heads):
    del prev_ref
    s_len = q_ref.shape[0]
    la = kc_ref.shape[1]

    @pl.when(pl.program_id(0) == 0)
    def _():
        for h in range(heads):
            bias_ref[h] = _toeplitz_bias(tbl_ref, (layer * heads + h) * TABLE, s_len, bias_ref.shape[2])

    scale = HEAD_DIM ** -0.5
    hs = range(heads)
    sl = [slice(h * HEAD_DIM, (h + 1) * HEAD_DIM) for h in hs]
    q = [q_ref[:, sl[h]].astype(BF16) for h in hs]
    sc = [_nt_dot(q[h], kc_ref[h].astype(BF16)) * scale + bias_ref[h, :, 0:la] for h in hs]
    sn = [_nt_dot(q[h], kn_ref[:, sl[h]].astype(BF16)) * scale + bias_ref[h, :, la:la + s_len] for h in hs]
    m = [jnp.maximum(jnp.max(sc[h], axis=-1, keepdims=True), jnp.max(sn[h], axis=-1, keepdims=True)) for h in hs]
    pc = [jnp.exp(sc[h] - m[h]) for h in hs]
    pn = [jnp.exp(sn[h] - m[h]) for h in hs]
    l = [jnp.sum(pc[h], axis=-1, keepdims=True) + jnp.sum(pn[h], axis=-1, keepdims=True) for h in hs]
    o = [jnp.dot(pc[h].astype(BF16), vc_ref[h].astype(BF16), preferred_element_type=F32)
         + jnp.dot(pn[h].astype(BF16), vn_ref[:, sl[h]].astype(BF16), preferred_element_type=F32) for h in hs]
    for h in hs:
        o_ref[:, sl[h]] = (o[h] / l[h]).astype(o_ref.dtype)


def _attn_sample(z, cache_k, cache_v, table, prev, layer, *, row0, batch, s_len, heads):
    la = cache_k.shape[3]
    assert la == BAND, "the relative-position bias tile assumes a full band of cached rows"
    width = heads * HEAD_DIM
    rb0 = row0 // s_len
    bias_w = pl.cdiv(la + 2 * s_len, LANES) * LANES
    new_spec = lambda c: pl.BlockSpec((s_len, width), lambda b: (rb0 + b, c))
    cache_spec = pl.BlockSpec((None, None, heads, la, HEAD_DIM), lambda b: (layer, b, 0, 0, 0))
    return pl.pallas_call(
        functools.partial(_attn_sample_kernel, layer=layer, heads=heads),
        grid=(batch,),
        in_specs=[pl.BlockSpec(memory_space=pltpu.SMEM), new_spec(0), new_spec(1), new_spec(2),
                  cache_spec, cache_spec, pl.BlockSpec(memory_space=pl.ANY)],
        out_specs=pl.BlockSpec((s_len, width), lambda b: (rb0 + b, 0)),
        out_shape=jax.ShapeDtypeStruct(prev.shape, prev.dtype),
        scratch_shapes=[pltpu.VMEM((heads, s_len, bias_w), F32)],
        input_output_aliases={6: 0},
        compiler_params=_params("arbitrary"),
        name="band_attention_sample",
    )(table.reshape(-1), z, z, z, cache_k, cache_v, prev)


def _ret_kernel(lg_ref, gb_ref, q_ref, k_ref, v_ref, gate_ref, cos_ref, sin_ref, gain_ref, s0_ref, *rest,
                cb, nblk, hp, aliased):
    if aliased:
        rest = rest[1:]
    o_ref, sfin_ref = rest[0], rest[1]
    hgroup = pl.program_id(1)
    row = lax.broadcasted_iota(jnp.int32, (cb, cb), 0)
    col = lax.broadcasted_iota(jnp.int32, (cb, cb), 1)
    diff = (row - col).astype(F32)
    t = lax.broadcasted_iota(jnp.int32, (cb, HEAD_DIM), 0).astype(F32)

    for hh in range(hp):
        head = hgroup * hp + hh
        log_g = lg_ref[head]
        g_block = gb_ref[head]
        sl = slice(hh * HEAD_DIM, (hh + 1) * HEAD_DIM)
        key_scale = HEAD_DIM ** -0.5
        decay = jnp.where(diff >= 0, jnp.exp(jnp.maximum(diff, 0.0) * log_g) * key_scale, 0.0)
        if nblk > 1:
            decay_ref = rest[2]
            decay_ref[...] = decay
        q_w = jnp.exp((t + 1.0) * log_g)
        k_w = jnp.exp((cb - 1.0 - t) * log_g) * key_scale
        gain = gain_ref[:, sl]

        def block(n, state, sl=sl, q_w=q_w, k_w=k_w, gain=gain, decay=decay):
            r = pl.multiple_of(n * cb, cb)
            rows = pl.ds(r, cb)
            cos, sin = cos_ref[rows, :], sin_ref[rows, :]

            def rot(x):
                return x * cos + pltpu.roll(x, HEAD_DIM // 2, 1) * sin

            q = rot(q_ref[rows, sl])
            k = rot(k_ref[rows, sl])
            v = v_ref[rows, sl].astype(BF16)
            qb = q.astype(BF16)
            dec = rest[2][...] if nblk > 1 else decay
            scores = _nt_dot(qb, k.astype(BF16)) * dec
            o = jnp.dot(scores.astype(BF16), v, preferred_element_type=F32)
            o += jnp.dot(qb, state.astype(BF16), preferred_element_type=F32) * q_w
            kv = lax.dot_general((k * k_w).astype(BF16), v, (((0,), (0,)), ((), ())), preferred_element_type=F32)
            mu = jnp.mean(o, axis=-1, keepdims=True)
            var = jnp.mean(jnp.square(o - mu), axis=-1, keepdims=True)
            on = (o - mu) * lax.rsqrt(var + EPS) * gain
            gate = gate_ref[rows, sl]
            o_ref[rows, sl] = (gate * jax.nn.sigmoid(gate) * on).astype(o_ref.dtype)
            return state * g_block + kv

        state = s0_ref[hh]
        if nblk > 1:
            state = lax.fori_loop(0, nblk, block, state, unroll=LOOP_UNROLL)
        else:
            state = block(0, state)
        sfin_ref[hh] = state


def _retention(z, s0, gain, pos, prev, layer, *, row0, batch, seq, cb, hp, heads, col0, t_total):
    nblk = seq // cb
    assert seq % cb == 0 and heads % hp == 0 and row0 % seq == 0 and col0 % hp == 0
    rb0 = row0 // seq
    width = heads * HEAD_DIM
    hidx = jnp.arange(heads, dtype=F32)
    log_g = jnp.log1p(-jnp.exp2(-5.0 - hidx))
    g_block = jnp.exp(cb * log_g)
    half = HEAD_DIM // 2
    inv = ROPE_BASE ** (-jnp.arange(half, dtype=F32) / half)
    ang = pos[:, None] * inv[None, :]
    cos = jnp.concatenate([jnp.cos(ang), jnp.cos(ang)], axis=-1)
    sin = jnp.concatenate([-jnp.sin(ang), jnp.sin(ang)], axis=-1)

    bw = hp * HEAD_DIM
    zspec = lambda c: pl.BlockSpec((seq, bw), lambda b, h: (rb0 + b, (c + h * hp) // hp))
    rope_spec = pl.BlockSpec((seq, HEAD_DIM), lambda b, h: (0, 0))
    smem = pl.BlockSpec(memory_space=pltpu.SMEM)
    if s0.ndim == 5:
        s0_spec = pl.BlockSpec((None, None, hp, HEAD_DIM, HEAD_DIM), lambda b, h: (layer, b, h, 0, 0))
    else:
        s0_spec = pl.BlockSpec((None, hp, HEAD_DIM, HEAD_DIM), lambda b, h: (b, h, 0, 0))
    gain_spec = pl.BlockSpec((None, 1, bw), lambda b, h: (layer, 0, h))
    in_specs = [smem, smem, zspec(col0), zspec(col0 + heads), zspec(col0 + 2 * heads), zspec(col0 + 3 * heads),
                rope_spec, rope_spec, gain_spec, s0_spec]
    args = [log_g, g_block, z, z, z, z, cos, sin, gain, s0]
    aliases = {}
    if prev is not None:
        in_specs.append(pl.BlockSpec(memory_space=pl.ANY))
        args.append(prev)
        aliases = {len(args) - 1: 0}
    return pl.pallas_call(
        functools.partial(_ret_kernel, cb=cb, nblk=nblk, hp=hp, aliased=prev is not None),
        grid=(batch, heads // hp),
        in_specs=in_specs,
        out_specs=[pl.BlockSpec((seq, bw), lambda b, h: (rb0 + b, h)),
                   pl.BlockSpec((None, hp, HEAD_DIM, HEAD_DIM), lambda b, h: (b, h, 0, 0))],
        out_shape=[jax.ShapeDtypeStruct((t_total, width), BF16),
                   jax.ShapeDtypeStruct((batch, heads, HEAD_DIM, HEAD_DIM), F32)],
        scratch_shapes=[pltpu.VMEM((cb, cb), F32)] if nblk > 1 else [],
        input_output_aliases=aliases,
        compiler_params=_params("arbitrary", "arbitrary"),
        name="retention_prompt" if prev is None else "retention_sample",
    )(*args)


def _pool_kernel(u_ref, halo_ref, w_ref, sc_ref, *rest, pos0, nseq, first_tile_has_no_history, aliased):
    if aliased:
        rest = rest[1:]
    o_ref, x_ref = rest
    i, g = pl.program_id(1), pl.program_id(2)
    tr = u_ref.shape[0]
    sr = tr // nseq
    for s in range(nseq):
        halo = halo_ref[s] if nseq > 1 else halo_ref[...]
        if first_tile_has_no_history:
            halo = jnp.where(i == 0, 0.0, halo)
        x_ref[s, 0:POOL_HALO, :] = halo
        x_ref[s, POOL_HALO:POOL_HALO + sr, :] = u_ref[s * sr:(s + 1) * sr, :]
    pos = (pos0 + i * sr + lax.broadcasted_iota(jnp.int32, (sr, 1), 0)).astype(F32)

    for gi, w in enumerate(POOL_WINDOWS):
        @pl.when(g == gi)
        def _(w=w):
            inv_cnt = 1.0 / jnp.minimum(float(w), pos + 1.0)
            parts = []
            for s in range(nseq):
                win = x_ref[s, POOL_HALO:POOL_HALO + sr, :]
                for j in range(1, w):
                    win = win + x_ref[s, POOL_HALO - j:POOL_HALO - j + sr, :]
                parts.append(win * inv_cnt - x_ref[s, POOL_HALO:POOL_HALO + sr, :])
            diff = parts[0] if nseq == 1 else jnp.concatenate(parts, axis=0)
            y = jnp.dot(diff.astype(BF16), w_ref[...].astype(BF16), preferred_element_type=F32)
            o_ref[...] = (y * sc_ref[...]).astype(o_ref.dtype)


def _pool(z, halo, w_pool, scale, prev, layer, *, row0, batch, seq, tr, pos0, col0, t_total):
    groups, gw = w_pool.shape[1], w_pool.shape[2]
    assert groups == len(POOL_WINDOWS) and max(POOL_WINDOWS) - 1 <= POOL_HALO and tr % POOL_HALO == 0
    in_specs = []
    if halo is None:
        nseq, nt, nb = 1, seq // tr, batch
        rb0 = row0 // tr
        per = tr // POOL_HALO
        in_specs += [pl.BlockSpec((tr, gw), lambda b, i, g: (rb0 + b * nt + i, col0 + g)),
                     pl.BlockSpec((POOL_HALO, gw),
                                  lambda b, i, g: (jnp.maximum((rb0 + b * nt + i) * per - 1, 0), col0 + g))]
        args = [z, z]
    else:
        assert tr == batch * seq and row0 % tr == 0
        nseq, nt, nb = batch, 1, 1
        rb0 = row0 // tr
        in_specs += [pl.BlockSpec((tr, gw), lambda b, i, g: (rb0, col0 + g)),
                     pl.BlockSpec((None, batch, POOL_HALO, gw), lambda b, i, g: (layer, 0, 0, g))]
        args = [z, halo]
    in_specs += [pl.BlockSpec((None, None, gw, gw), lambda b, i, g: (layer, g, 0, 0)),
                 pl.BlockSpec((None, 1, gw), lambda b, i, g: (layer, 0, g))]
    args += [w_pool, scale]
    aliases = {}
    if prev is not None:
        in_specs.append(pl.BlockSpec(memory_space=pl.ANY))
        args.append(prev)
        aliases = {len(args) - 1: 0}
    return pl.pallas_call(
        functools.partial(_pool_kernel, pos0=pos0, nseq=nseq, first_tile_has_no_history=halo is None,
                          aliased=prev is not None),
        grid=(nb, nt, groups),
        in_specs=in_specs,
        out_specs=pl.BlockSpec((tr, gw), lambda b, i, g: (rb0 + b * nt + i, g)),
        out_shape=jax.ShapeDtypeStruct((t_total, groups * gw), BF16),
        scratch_shapes=[pltpu.VMEM((nseq, POOL_HALO + tr // nseq, gw), F32)],
        input_output_aliases=aliases,
        compiler_params=_params("arbitrary", "arbitrary", "arbitrary"),
        name="pool_prompt" if prev is None else "pool_sample",
    )(*args)


def kernel(x_prompt, x_sample, cache_band_k, cache_band_v, state_retention, state_pool, norm_mix, w_in,
           rel_bias_table, ret_norm, pool_w, pool_scale, w_out, norm_ffn, w_up, w_down, norm_final):
    bp, lp, d = x_prompt.shape
    bs, ls, _ = x_sample.shape
    depth = w_in.shape[0]
    a_heads = rel_bias_table.shape[1]
    a_width = a_heads * HEAD_DIM
    b_width = ret_norm.shape[1]
    b_heads = b_width // HEAD_DIM
    gw = pool_w.shape[2]
    tp, ts = bp * lp, bs * ls
    t_total = tp + ts
    keep = min(BAND, lp)
    pool_ctx = state_pool.shape[2]
    assert a_width % gw == 0 and (3 * a_width + 4 * b_width) % gw == 0 and tp % ts == 0
    bq_col = 3 * a_heads
    cu_col = (3 * a_width + 4 * b_width) // gw
    cu0 = cu_col * gw

    pos_p = jnp.arange(lp, dtype=F32)
    pos_s = PAST_LEN + jnp.arange(ls, dtype=F32)
    zero_state = jnp.zeros((bp, b_heads, HEAD_DIM, HEAD_DIM), F32)
    pool_halo = jnp.pad(state_pool, ((0, 0), (0, 0), (POOL_HALO - pool_ctx, 0), (0, 0)))
    cache_k = jnp.transpose(cache_band_k, (0, 1, 3, 2, 4))
    cache_v = jnp.transpose(cache_band_v, (0, 1, 3, 2, 4))
    norm_mix, norm_ffn, ret_norm, pool_scale = (
        g.reshape(depth, 1, -1) for g in (norm_mix, norm_ffn, ret_norm, pool_scale))

    def tail(z, b, rows, c0, c1):
        return lax.slice(z, ((b + 1) * lp - rows, c0), ((b + 1) * lp, c1))

    kp_l, vp_l, rp_l, pp_l, ks_l, vs_l, rs_l, ps_l = [], [], [], [], [], [], [], []
    for l in range(depth):
        if l == 0:
            h, xn = _stage([x_prompt.reshape(tp, d), x_sample.reshape(ts, d)], norm_mix, l)
        else:
            xn, = _stage([h], norm_mix, l)
        z = _wide_matmul(xn, w_in, l, relu2=False, out_dtype=F32)

        a_o = _attn_prompt(z, rel_bias_table, l, batch=bp, seq=lp, heads=a_heads, t_total=t_total)
        a_o = _attn_sample(z, cache_k, cache_v, rel_bias_table, a_o, l,
                           row0=tp, batch=bs, s_len=ls, heads=a_heads)
        b_o, s_fin = _retention(z, zero_state, ret_norm, pos_p, None, l, row0=0, batch=bp, seq=lp,
                                cb=RET_BLOCK, hp=1, heads=b_heads, col0=bq_col, t_total=t_total)
        b_o, s_new = _retention(z, state_retention, ret_norm, pos_s, b_o, l, row0=tp, batch=bs, seq=ls,
                                cb=ls, hp=b_heads, heads=b_heads, col0=bq_col, t_total=t_total)
        c_o = _pool(z, None, pool_w, pool_scale, None, l, row0=0, batch=bp, seq=lp,
                    tr=_pick_tile(lp, POOL_ROWS, POOL_HALO), pos0=0, col0=cu_col, t_total=t_total)
        c_o = _pool(z, pool_halo, pool_w, pool_scale, c_o, l, row0=tp, batch=bs, seq=ls,
                    tr=ts, pos0=PAST_LEN, col0=cu_col, t_total=t_total)

        h, hg, ssq = _outproj(h, a_o, b_o, c_o, w_out, norm_ffn, l)
        act = _wide_matmul(hg, w_up, l, relu2=True, out_dtype=BF16)
        h = _down_residual(h, ssq, act, w_down, l)

        kp_l.append(jnp.stack([tail(z, b, keep, a_width, 2 * a_width) for b in range(bp)])
                    .reshape(bp, keep, a_heads, HEAD_DIM))
        vp_l.append(jnp.stack([tail(z, b, keep, 2 * a_width, 3 * a_width) for b in range(bp)])
                    .reshape(bp, keep, a_heads, HEAD_DIM))
        rp_l.append(s_fin)
        pp_l.append(jnp.stack([tail(z, b, pool_ctx, cu0, z.shape[1]) for b in range(bp)]))
        zs_kv = lax.slice(z, (tp, a_width), (t_total, 3 * a_width)).reshape(bs, ls, 2, a_heads, HEAD_DIM)
        ks_l.append(zs_kv[:, :, 0])
        vs_l.append(zs_kv[:, :, 1])
        rs_l.append(s_new)
        cu_s = lax.slice(z, (tp, cu0), (t_total, z.shape[1])).reshape(bs, ls, -1)
        ps_l.append(jnp.concatenate([state_pool[l], cu_s], axis=1)[:, -pool_ctx:])

    y_prompt = _rmsnorm(h, norm_final, row0=0, rows=tp)
    y_sample = _rmsnorm(h, norm_final, row0=tp, rows=ts)
    return (y_prompt.reshape(bp, lp, d), y_sample.reshape(bs, ls, d),
            jnp.stack(kp_l), jnp.stack(vp_l), jnp.stack(rp_l), jnp.stack(pp_l),
            jnp.stack(ks_l), jnp.stack(vs_l), jnp.stack(rs_l), jnp.stack(ps_l))
```

```python
import functools
import math

import jax
import jax.numpy as jnp
from jax import lax
from jax.experimental import pallas as pl
from jax.experimental.pallas import tpu as pltpu

F32 = jnp.float32
BF16 = jnp.bfloat16

PAST_LEN = 1024
CHUNK = 64
BAND_CHUNKS = 8
BAND = BAND_CHUNKS * CHUNK
HEAD_DIM = 128
REL_CLIP = 128
TABLE = 2 * REL_CLIP + 1
POOL_WINDOWS = (2, 4, 8, 16)
POOL_HALO = 16
ROPE_BASE = 10000.0
EPS = 1e-6
NEG_INF = -1e30
LOG2_E = 1.4426950408889634

LANES = 128
SUBLANES_BF16 = 16
MXU_COLS = 256
VMEM_LIMIT_BYTES = 60 * 1024 * 1024

ATTN_TQ = 256
RET_BLOCK = 256
ATTN_UNROLL = 7
RET_UNROLL = 4
STAGE_ROWS = 256
NORM_ROWS = 64
ROW_TILE = 1100
WIDE_ROW_TILE = 2200
POOL_ROWS = 2048


def _pick_tile(n, target, align):
    best = None
    for t in range(align, min(n, target) + 1, align):
        if n % t == 0:
            best = t
    if best is None:
        raise ValueError(f"no tile for {n} (align {align}, target {target})")
    return best


def _params(*sem):
    return pltpu.CompilerParams(dimension_semantics=sem, vmem_limit_bytes=VMEM_LIMIT_BYTES)


def _fold_lanes(y):
    return functools.reduce(jnp.add, [y[:, c:c + LANES] for c in range(0, y.shape[1], LANES)])


def _stage_kernel(*refs, n_first, copy):
    n_out = 2 if copy else 1
    x_refs, g_ref, outs = refs[:-n_out - 1], refs[-n_out - 1], refs[-n_out:]

    def emit(x_ref):
        def body(c, carry):
            rows = pl.ds(pl.multiple_of(c * NORM_ROWS, NORM_ROWS), NORM_ROWS)
            x = x_ref[rows, :]
            if copy:
                outs[0][rows, :] = x
            ms = jnp.mean(x * x, axis=-1, keepdims=True)
            outs[-1][rows, :] = (x * lax.rsqrt(ms + EPS) * g_ref[...]).astype(BF16)
            return carry
        lax.fori_loop(0, x_ref.shape[0] // NORM_ROWS, body, 0)

    if len(x_refs) == 1:
        emit(x_refs[0])
    else:
        @pl.when(pl.program_id(0) < n_first)
        def _():
            emit(x_refs[0])

        @pl.when(pl.program_id(0) >= n_first)
        def _():
            emit(x_refs[1])


def _stage(xs, g, layer):
    d = xs[0].shape[1]
    tm = _pick_tile(math.gcd(*[x.shape[0] for x in xs]), STAGE_ROWS, NORM_ROWS)
    n_first = xs[0].shape[0] // tm
    t = sum(x.shape[0] for x in xs)
    copy = len(xs) > 1
    in_specs = [pl.BlockSpec((tm, d), lambda i: (jnp.minimum(i, n_first - 1), 0))]
    if copy:
        in_specs.append(pl.BlockSpec((tm, d), lambda i: (jnp.maximum(i - n_first, 0), 0)))
    in_specs.append(pl.BlockSpec((None, 1, d), lambda i: (layer, 0, 0)))
    out_specs = [pl.BlockSpec((tm, d), lambda i: (i, 0))]
    out_shape = [jax.ShapeDtypeStruct((t, d), BF16)]
    if copy:
        out_specs.insert(0, pl.BlockSpec((tm, d), lambda i: (i, 0)))
        out_shape.insert(0, jax.ShapeDtypeStruct((t, d), F32))
    return pl.pallas_call(
        functools.partial(_stage_kernel, n_first=n_first, copy=copy),
        grid=(t // tm,),
        in_specs=in_specs,
        out_specs=out_specs,
        out_shape=out_shape,
        compiler_params=_params("parallel"),
        name="stage_rows",
    )(*xs, g)


def _matmul_kernel(x_ref, w_ref, o_ref, *, relu2):
    for c in range(0, o_ref.shape[1], MXU_COLS):
        y = jnp.dot(x_ref[...], w_ref[:, c:c + MXU_COLS].astype(BF16), preferred_element_type=F32)
        if relu2:
            y = jnp.square(jnp.maximum(y, 0.0))
        o_ref[:, c:c + MXU_COLS] = y.astype(o_ref.dtype)


def _wide_matmul(x, w, layer, *, relu2, out_dtype):
    t, k = x.shape
    n = w.shape[2]
    tm = _pick_tile(t, WIDE_ROW_TILE, SUBLANES_BF16)
    tn = _pick_tile(n, 512, MXU_COLS)
    return pl.pallas_call(
        functools.partial(_matmul_kernel, relu2=relu2),
        grid=(t // tm, n // tn),
        in_specs=[pl.BlockSpec((tm, k), lambda i, j: (i, 0), pipeline_mode=pl.Buffered(1)),
                  pl.BlockSpec((None, k, tn), lambda i, j: (layer, 0, j))],
        out_specs=pl.BlockSpec((tm, tn), lambda i, j: (i, j)),
        out_shape=jax.ShapeDtypeStruct((t, n), out_dtype),
        compiler_params=_params("parallel", "parallel"),
        name="matmul_relu2" if relu2 else "matmul",
    )(x, w)


def _outproj_kernel(a_ref, b_ref, c_ref, wa_ref, wb_ref, wc_ref, h_ref, g_ref, o_ref, hg_ref, ssq_ref):
    ssq = None
    for c in range(0, o_ref.shape[1], MXU_COLS):
        cols = slice(c, c + MXU_COLS)
        acc = jnp.dot(a_ref[...], wa_ref[:, cols].astype(BF16), preferred_element_type=F32)
        acc += jnp.dot(b_ref[...], wb_ref[:, cols].astype(BF16), preferred_element_type=F32)
        acc += jnp.dot(c_ref[...], wc_ref[:, cols].astype(BF16), preferred_element_type=F32)
        hn = h_ref[:, cols] + acc
        o_ref[:, cols] = hn
        hg_ref[:, cols] = (hn * g_ref[:, cols]).astype(BF16)
        part = _fold_lanes(hn * hn)
        ssq = part if ssq is None else ssq + part

    @pl.when(pl.program_id(1) == 0)
    def _():
        ssq_ref[...] = ssq

    @pl.when(pl.program_id(1) > 0)
    def _():
        ssq_ref[...] += ssq


def _outproj(h, a_o, b_o, c_o, w_out, g, layer):
    t, d = h.shape
    wa, wb, wc = a_o.shape[1], b_o.shape[1], c_o.shape[1]
    assert wa == wb and (wa + wb) % wc == 0
    tm = _pick_tile(t, ROW_TILE, SUBLANES_BF16)
    tn = _pick_tile(d, 512, MXU_COLS)
    return pl.pallas_call(
        _outproj_kernel,
        grid=(t // tm, d // tn),
        in_specs=[
            pl.BlockSpec((tm, wa), lambda i, j: (i, 0)),
            pl.BlockSpec((tm, wb), lambda i, j: (i, 0)),
            pl.BlockSpec((tm, wc), lambda i, j: (i, 0)),
            pl.BlockSpec((None, wa, tn), lambda i, j: (layer, 0, j)),
            pl.BlockSpec((None, wb, tn), lambda i, j: (layer, 1, j)),
            pl.BlockSpec((None, wc, tn), lambda i, j: (layer, (wa + wb) // wc, j)),
            pl.BlockSpec((tm, tn), lambda i, j: (i, j)),
            pl.BlockSpec((None, 1, tn), lambda i, j: (layer, 0, j)),
        ],
        out_specs=[pl.BlockSpec((tm, tn), lambda i, j: (i, j)),
                   pl.BlockSpec((tm, tn), lambda i, j: (i, j)),
                   pl.BlockSpec((tm, LANES), lambda i, j: (i, 0))],
        out_shape=[jax.ShapeDtypeStruct((t, d), F32), jax.ShapeDtypeStruct((t, d), BF16),
                   jax.ShapeDtypeStruct((t, LANES), F32)],
        compiler_params=_params("parallel", "arbitrary"),
        name="outproj_residual",
    )(a_o, b_o, c_o, w_out, w_out, w_out, h, g)


def _down_kernel(a_ref, w_ref, h_ref, ssq_ref, o_ref, *, nk, norm_dim):
    k = pl.program_id(2)
    col_chunks = [slice(c, c + MXU_COLS) for c in range(0, o_ref.shape[1], MXU_COLS)]

    def partial_product(cols):
        return jnp.dot(a_ref[...], w_ref[:, cols].astype(BF16), preferred_element_type=F32)

    @pl.when(k == 0)
    def _():
        for cols in col_chunks:
            o_ref[:, cols] = partial_product(cols)

    @pl.when((k > 0) & (k < nk - 1))
    def _():
        for cols in col_chunks:
            o_ref[:, cols] += partial_product(cols)

    @pl.when(k == nk - 1)
    def _():
        ms = jnp.sum(ssq_ref[...], axis=-1, keepdims=True) * (1.0 / norm_dim)
        r2 = 1.0 / (ms + EPS)
        for cols in col_chunks:
            o_ref[:, cols] = h_ref[:, cols] + (o_ref[:, cols] + partial_product(cols)) * r2


def _down_residual(h, ssq, a, w, layer):
    t, d = h.shape
    f = a.shape[1]
    tm = _pick_tile(t, WIDE_ROW_TILE, SUBLANES_BF16)
    tn = _pick_tile(d, 1024, MXU_COLS)
    tk = _pick_tile(f, 1024, LANES)
    assert f // tk >= 2
    return pl.pallas_call(
        functools.partial(_down_kernel, nk=f // tk, norm_dim=d),
        grid=(t // tm, d // tn, f // tk),
        in_specs=[
            pl.BlockSpec((tm, tk), lambda i, j, k: (i, k)),
            pl.BlockSpec((None, tk, tn), lambda i, j, k: (layer, k, j)),
            pl.BlockSpec((tm, tn), lambda i, j, k: (i, j)),
            pl.BlockSpec((tm, LANES), lambda i, j, k: (i, 0), pipeline_mode=pl.Buffered(1)),
        ],
        out_specs=pl.BlockSpec((tm, tn), lambda i, j, k: (i, j)),
        out_shape=jax.ShapeDtypeStruct((t, d), F32),
        compiler_params=_params("parallel", "parallel", "arbitrary"),
        name="down_residual",
    )(a, w, h, ssq)


def _rmsnorm_kernel(x_ref, g_ref, o_ref):
    x = x_ref[...]
    ms = jnp.mean(x * x, axis=-1, keepdims=True)
    o_ref[...] = x * lax.rsqrt(ms + EPS) * g_ref[...]


def _rmsnorm(x, g, *, row0, rows):
    d = x.shape[1]
    tm = _pick_tile(rows, 256, 8)
    assert row0 % tm == 0
    return pl.pallas_call(
        _rmsnorm_kernel,
        grid=(rows // tm,),
        in_specs=[pl.BlockSpec((tm, d), lambda i: (row0 // tm + i, 0)), pl.BlockSpec((1, d), lambda i: (0, 0))],
        out_specs=pl.BlockSpec((tm, d), lambda i: (i, 0)),
        out_shape=jax.ShapeDtypeStruct((rows, d), F32),
        compiler_params=_params("parallel"),
        name="final_rmsnorm",
    )(x, g.reshape(1, d))


def _toeplitz_bias(tbl_ref, base, rows, width):
    m = lax.broadcasted_iota(jnp.int32, (8, width), 1)
    idx = jnp.where(m >= BAND + rows, 2 * REL_CLIP, jnp.clip(BAND + REL_CLIP - m, 0, 2 * REL_CLIP))

    def body(d, g):
        return jnp.where(idx == d, tbl_ref[base + d], g)

    g = lax.fori_loop(0, TABLE, body, jnp.zeros((8, width), F32))
    full = jnp.broadcast_to(g[0:1, :], (rows, width))
    return pltpu.roll(full, 0, 1, stride=1, stride_axis=0)


def _nt_dot(a, b):
    return lax.dot_general(a, b, (((1,), (1,)), ((), ())), preferred_element_type=F32)


def _attn_prompt_kernel(tbl_ref, q_ref, k_ref, v_ref, o_ref, bias_ref, kb_ref, vb_ref, *, layer, heads, tq, nkb):
    head, b = pl.program_id(0), pl.program_id(1)
    seq = q_ref.shape[0]
    tk = nkb * tq

    @pl.when(b == 0)
    def _():
        bias = _toeplitz_bias(tbl_ref, (layer * heads + head) * TABLE, tq, tk + tq)[:, :tk]
        qc = lax.broadcasted_iota(jnp.int32, (tq, tk), 0) // CHUNK
        kc = lax.broadcasted_iota(jnp.int32, (tq, tk), 1) // CHUNK
        bias_ref[...] = jnp.where((kc >= qc) & (kc <= qc + BAND_CHUNKS), bias * LOG2_E, NEG_INF)

    kb_ref[...] = k_ref[...].astype(BF16)
    vb_ref[:, 0:HEAD_DIM] = v_ref[...].astype(BF16)
    vb_ref[:, HEAD_DIM:2 * HEAD_DIM] = jnp.ones((seq, HEAD_DIM), BF16)
    scale = HEAD_DIM ** -0.5 * LOG2_E

    def tile(q0, k0, nk, bias):
        q = (q_ref[pl.ds(q0, tq), :] * scale).astype(BF16)
        s = _nt_dot(q, kb_ref[pl.ds(k0, nk), :]) + bias
        m = jnp.max(s, axis=-1, keepdims=True)
        p = jnp.exp2(s - m)
        ol = jnp.dot(p.astype(BF16), vb_ref[pl.ds(k0, nk), :], preferred_element_type=F32)
        o_ref[pl.ds(q0, tq), :] = (ol[:, 0:HEAD_DIM] / ol[:, HEAD_DIM:2 * HEAD_DIM]).astype(o_ref.dtype)

    for i in range(nkb - 1):
        nk = (i + 1) * tq
        tile(i * tq, 0, nk, bias_ref[:, tk - nk:tk])

    def body(i, carry):
        q0 = pl.multiple_of(i * tq, tq)
        k0 = pl.multiple_of((i - (nkb - 1)) * tq, tq)
        tile(q0, k0, tk, bias_ref[...])
        return carry

    lax.fori_loop(nkb - 1, seq // tq, body, 0, unroll=ATTN_UNROLL)


def _attn_prompt(z, table, layer, *, batch, seq, heads, t_total):
    tq = ATTN_TQ
    assert seq % tq == 0 and BAND % tq == 0 and tq % CHUNK == 0 and seq // tq >= BAND // tq + 1
    nkb = BAND // tq + 1
    width = heads * HEAD_DIM
    spec = lambda c: pl.BlockSpec((seq, HEAD_DIM), lambda h, b: (b, c + h))
    return pl.pallas_call(
        functools.partial(_attn_prompt_kernel, layer=layer, heads=heads, tq=tq, nkb=nkb),
        grid=(heads, batch),
        in_specs=[pl.BlockSpec(memory_space=pltpu.SMEM), spec(0), spec(heads), spec(2 * heads)],
        out_specs=pl.BlockSpec((seq, HEAD_DIM), lambda h, b: (b, h)),
        out_shape=jax.ShapeDtypeStruct((t_total, width), BF16),
        scratch_shapes=[pltpu.VMEM((tq, nkb * tq), F32), pltpu.VMEM((seq, HEAD_DIM), BF16),
                        pltpu.VMEM((seq, 2 * HEAD_DIM), BF16)],
        compiler_params=_params("arbitrary", "arbitrary"),
        name="band_attention_prompt",
    )(table.reshape(-1), z, z, z)


def _attn_sample_kernel(tbl_ref, q_ref, kn_ref, vn_ref, kc_ref, vc_ref, prev_ref, o_ref, bias_ref, *, layer, heads):
    del prev_ref
    s_len = q_ref.shape[0]
    la = kc_ref.shape[1]

    @pl.when(pl.program_id(0) == 0)
    def _():
        for h in range(heads):
            bias_ref[h] = _toeplitz_bias(tbl_ref, (layer * heads + h) * TABLE, s_len, bias_ref.shape[2])

    scale = HEAD_DIM ** -0.5
    hs = range(heads)
    sl = [slice(h * HEAD_DIM, (h + 1) * HEAD_DIM) for h in hs]
    q = [q_ref[:, sl[h]].astype(BF16) for h in hs]
    sc = [_nt_dot(q[h], kc_ref[h].astype(BF16)) * scale + bias_ref[h, :, 0:la] for h in hs]
    sn = [_nt_dot(q[h], kn_ref[:, sl[h]].astype(BF16)) * scale + bias_ref[h, :, la:la + s_len] for h in hs]
    m = [jnp.maximum(jnp.max(sc[h], axis=-1, keepdims=True), jnp.max(sn[h], axis=-1, keepdims=True)) for h in hs]
    pc = [jnp.exp(sc[h] - m[h]) for h in hs]
    pn = [jnp.exp(sn[h] - m[h]) for h in hs]
    l = [jnp.sum(pc[h], axis=-1, keepdims=True) + jnp.sum(pn[h], axis=-1, keepdims=True) for h in hs]
    o = [jnp.dot(pc[h].astype(BF16), vc_ref[h].astype(BF16), preferred_element_type=F32)
         + jnp.dot(pn[h].astype(BF16), vn_ref[:, sl[h]].astype(BF16), preferred_element_type=F32) for h in hs]
    for h in hs:
        o_ref[:, sl[h]] = (o[h] / l[h]).astype(o_ref.dtype)


def _attn_sample(z, cache_k, cache_v, table, prev, layer, *, row0, batch, s_len, heads):
    la = cache_k.shape[3]
    assert la == BAND, "the relative-position bias tile assumes a full band of cached rows"
    width = heads * HEAD_DIM
    rb0 = row0 // s_len
    bias_w = pl.cdiv(la + 2 * s_len, LANES) * LANES
    new_spec = lambda c: pl.BlockSpec((s_len, width), lambda b: (rb0 + b, c))
    cache_spec = pl.BlockSpec((None, None, heads, la, HEAD_DIM), lambda b: (layer, b, 0, 0, 0))
    return pl.pallas_call(
        functools.partial(_attn_sample_kernel, layer=layer, heads=heads),
        grid=(batch,),
        in_specs=[pl.BlockSpec(memory_space=pltpu.SMEM), new_spec(0), new_spec(1), new_spec(2),
                  cache_spec, cache_spec, pl.BlockSpec(memory_space=pl.ANY)],
        out_specs=pl.BlockSpec((s_len, width), lambda b: (rb0 + b, 0)),
        out_shape=jax.ShapeDtypeStruct(prev.shape, prev.dtype),
        scratch_shapes=[pltpu.VMEM((heads, s_len, bias_w), F32)],
        input_output_aliases={6: 0},
        compiler_params=_params("arbitrary"),
        name="band_attention_sample",
    )(table.reshape(-1), z, z, z, cache_k, cache_v, prev)


def _ret_kernel(lg_ref, gb_ref, q_ref, k_ref, v_ref, gate_ref, cos_ref, sin_ref, gain_ref, s0_ref, *rest,
                cb, nblk, hp, aliased):
    if aliased:
        rest = rest[1:]
    o_ref, sfin_ref = rest[0], rest[1]
    hgroup = pl.program_id(1)
    row = lax.broadcasted_iota(jnp.int32, (cb, cb), 0)
    col = lax.broadcasted_iota(jnp.int32, (cb, cb), 1)
    diff = (row - col).astype(F32)
    t = lax.broadcasted_iota(jnp.int32, (cb, HEAD_DIM), 0).astype(F32)

    for hh in range(hp):
        head = hgroup * hp + hh
        log_g = lg_ref[head]
        g_block = gb_ref[head]
        sl = slice(hh * HEAD_DIM, (hh + 1) * HEAD_DIM)
        key_scale = HEAD_DIM ** -0.5
        decay = jnp.where(diff >= 0, jnp.exp(jnp.maximum(diff, 0.0) * log_g) * key_scale, 0.0)
        if nblk > 1:
            decay_ref = rest[2]
            decay_ref[...] = decay
        q_w = jnp.exp((t + 1.0) * log_g)
        k_w = jnp.exp((cb - 1.0 - t) * log_g) * key_scale
        gain = gain_ref[:, sl]

        def block(n, state, sl=sl, q_w=q_w, k_w=k_w, gain=gain, decay=decay):
            r = pl.multiple_of(n * cb, cb)
            rows = pl.ds(r, cb)
            cos, sin = cos_ref[rows, :], sin_ref[rows, :]

            def rot(x):
                return x * cos + pltpu.roll(x, HEAD_DIM // 2, 1) * sin

            q = rot(q_ref[rows, sl])
            k = rot(k_ref[rows, sl])
            v = v_ref[rows, sl].astype(BF16)
            qb = q.astype(BF16)
            dec = rest[2][...] if nblk > 1 else decay
            scores = _nt_dot(qb, k.astype(BF16)) * dec
            o = jnp.dot(scores.astype(BF16), v, preferred_element_type=F32)
            o += jnp.dot(qb, state.astype(BF16), preferred_element_type=F32) * q_w
            kv = lax.dot_general((k * k_w).astype(BF16), v, (((0,), (0,)), ((), ())), preferred_element_type=F32)
            mu = jnp.mean(o, axis=-1, keepdims=True)
            var = jnp.mean(jnp.square(o - mu), axis=-1, keepdims=True)
            on = (o - mu) * lax.rsqrt(var + EPS) * gain
            gate = gate_ref[rows, sl]
            o_ref[rows, sl] = (gate * jax.nn.sigmoid(gate) * on).astype(o_ref.dtype)
            return state * g_block + kv

        state = s0_ref[hh]
        if nblk > 1:
            state = lax.fori_loop(0, nblk, block, state, unroll=RET_UNROLL)
        else:
            state = block(0, state)
        sfin_ref[hh] = state


def _retention(z, s0, gain, pos, prev, layer, *, row0, batch, seq, cb, hp, heads, col0, t_total):
    nblk = seq // cb
    assert seq % cb == 0 and heads % hp == 0 and row0 % seq == 0 and col0 % hp == 0
    rb0 = row0 // seq
    width = heads * HEAD_DIM
    hidx = jnp.arange(heads, dtype=F32)
    log_g = jnp.log1p(-jnp.exp2(-5.0 - hidx))
    g_block = jnp.exp(cb * log_g)
    half = HEAD_DIM // 2
    inv = ROPE_BASE ** (-jnp.arange(half, dtype=F32) / half)
    ang = pos[:, None] * inv[None, :]
    cos = jnp.concatenate([jnp.cos(ang), jnp.cos(ang)], axis=-1)
    sin = jnp.concatenate([-jnp.sin(ang), jnp.sin(ang)], axis=-1)

    bw = hp * HEAD_DIM
    zspec = lambda c: pl.BlockSpec((seq, bw), lambda b, h: (rb0 + b, (c + h * hp) // hp))
    rope_spec = pl.BlockSpec((seq, HEAD_DIM), lambda b, h: (0, 0))
    smem = pl.BlockSpec(memory_space=pltpu.SMEM)
    if s0.ndim == 5:
        s0_spec = pl.BlockSpec((None, None, hp, HEAD_DIM, HEAD_DIM), lambda b, h: (layer, b, h, 0, 0))
    else:
        s0_spec = pl.BlockSpec((None, hp, HEAD_DIM, HEAD_DIM), lambda b, h: (b, h, 0, 0))
    gain_spec = pl.BlockSpec((None, 1, bw), lambda b, h: (layer, 0, h))
    in_specs = [smem, smem, zspec(col0), zspec(col0 + heads), zspec(col0 + 2 * heads), zspec(col0 + 3 * heads),
                rope_spec, rope_spec, gain_spec, s0_spec]
    args = [log_g, g_block, z, z, z, z, cos, sin, gain, s0]
    aliases = {}
    if prev is not None:
        in_specs.append(pl.BlockSpec(memory_space=pl.ANY))
        args.append(prev)
        aliases = {len(args) - 1: 0}
    return pl.pallas_call(
        functools.partial(_ret_kernel, cb=cb, nblk=nblk, hp=hp, aliased=prev is not None),
        grid=(batch, heads // hp),
        in_specs=in_specs,
        out_specs=[pl.BlockSpec((seq, bw), lambda b, h: (rb0 + b, h)),
                   pl.BlockSpec((None, hp, HEAD_DIM, HEAD_DIM), lambda b, h: (b, h, 0, 0))],
        out_shape=[jax.ShapeDtypeStruct((t_total, width), BF16),
                   jax.ShapeDtypeStruct((batch, heads, HEAD_DIM, HEAD_DIM), F32)],
        scratch_shapes=[pltpu.VMEM((cb, cb), F32)] if nblk > 1 else [],
        input_output_aliases=aliases,
        compiler_params=_params("arbitrary", "arbitrary"),
        name="retention_prompt" if prev is None else "retention_sample",
    )(*args)


def _pool_kernel(u_ref, halo_ref, w_ref, sc_ref, *rest, pos0, nseq, first_tile_has_no_history, aliased):
    if aliased:
        rest = rest[1:]
    o_ref, x_ref = rest
    i, g = pl.program_id(1), pl.program_id(2)
    tr = u_ref.shape[0]
    sr = tr // nseq
    for s in range(nseq):
        halo = halo_ref[s] if nseq > 1 else halo_ref[...]
        if first_tile_has_no_history:
            halo = jnp.where(i == 0, 0.0, halo)
        x_ref[s, 0:POOL_HALO, :] = halo
        x_ref[s, POOL_HALO:POOL_HALO + sr, :] = u_ref[s * sr:(s + 1) * sr, :]
    pos = (pos0 + i * sr + lax.broadcasted_iota(jnp.int32, (sr, 1), 0)).astype(F32)

    for gi, w in enumerate(POOL_WINDOWS):
        @pl.when(g == gi)
        def _(w=w):
            inv_cnt = 1.0 / jnp.minimum(float(w), pos + 1.0)
            parts = []
            for s in range(nseq):
                win = x_ref[s, POOL_HALO:POOL_HALO + sr, :]
                for j in range(1, w):
                    win = win + x_ref[s, POOL_HALO - j:POOL_HALO - j + sr, :]
                parts.append(win * inv_cnt - x_ref[s, POOL_HALO:POOL_HALO + sr, :])
            diff = parts[0] if nseq == 1 else jnp.concatenate(parts, axis=0)
            y = jnp.dot(diff.astype(BF16), w_ref[...].astype(BF16), preferred_element_type=F32)
            o_ref[...] = (y * sc_ref[...]).astype(o_ref.dtype)


def _pool(z, halo, w_pool, scale, prev, layer, *, row0, batch, seq, tr, pos0, col0, t_total):
    groups, gw = w_pool.shape[1], w_pool.shape[2]
    assert groups == len(POOL_WINDOWS) and max(POOL_WINDOWS) - 1 <= POOL_HALO and tr % POOL_HALO == 0
    in_specs = []
    if halo is None:
        nseq, nt, nb = 1, seq // tr, batch
        rb0 = row0 // tr
        per = tr // POOL_HALO
        in_specs += [pl.BlockSpec((tr, gw), lambda b, i, g: (rb0 + b * nt + i, col0 + g)),
                     pl.BlockSpec((POOL_HALO, gw),
                                  lambda b, i, g: (jnp.maximum((rb0 + b * nt + i) * per - 1, 0), col0 + g))]
        args = [z, z]
    else:
        assert tr == batch * seq and row0 % tr == 0
        nseq, nt, nb = batch, 1, 1
        rb0 = row0 // tr
        in_specs += [pl.BlockSpec((tr, gw), lambda b, i, g: (rb0, col0 + g)),
                     pl.BlockSpec((None, batch, POOL_HALO, gw), lambda b, i, g: (layer, 0, 0, g))]
        args = [z, halo]
    in_specs += [pl.BlockSpec((None, None, gw, gw), lambda b, i, g: (layer, g, 0, 0)),
                 pl.BlockSpec((None, 1, gw), lambda b, i, g: (layer, 0, g))]
    args += [w_pool, scale]
    aliases = {}
    if prev is not None:
        in_specs.append(pl.BlockSpec(memory_space=pl.ANY))
        args.append(prev)
        aliases = {len(args) - 1: 0}
    return pl.pallas_call(
        functools.partial(_pool_kernel, pos0=pos0, nseq=nseq, first_tile_has_no_history=halo is None,
                          aliased=prev is not None),
        grid=(nb, nt, groups),
        in_specs=in_specs,
        out_specs=pl.BlockSpec((tr, gw), lambda b, i, g: (rb0 + b * nt + i, g)),
        out_shape=jax.ShapeDtypeStruct((t_total, groups * gw), BF16),
        scratch_shapes=[pltpu.VMEM((nseq, POOL_HALO + tr // nseq, gw), F32)],
        input_output_aliases=aliases,
        compiler_params=_params("arbitrary", "arbitrary", "arbitrary"),
        name="pool_prompt" if prev is None else "pool_sample",
    )(*args)


def kernel(x_prompt, x_sample, cache_band_k, cache_band_v, state_retention, state_pool, norm_mix, w_in,
           rel_bias_table, ret_norm, pool_w, pool_scale, w_out, norm_ffn, w_up, w_down, norm_final):
    bp, lp, d = x_prompt.shape
    bs, ls, _ = x_sample.shape
    depth = w_in.shape[0]
    a_heads = rel_bias_table.shape[1]
    a_width = a_heads * HEAD_DIM
    b_width = ret_norm.shape[1]
    b_heads = b_width // HEAD_DIM
    gw = pool_w.shape[2]
    tp, ts = bp * lp, bs * ls
    t_total = tp + ts
    keep = min(BAND, lp)
    pool_ctx = state_pool.shape[2]
    assert a_width % gw == 0 and (3 * a_width + 4 * b_width) % gw == 0 and tp % ts == 0
    bq_col = 3 * a_heads
    cu_col = (3 * a_width + 4 * b_width) // gw
    cu0 = cu_col * gw

    pos_p = jnp.arange(lp, dtype=F32)
    pos_s = PAST_LEN + jnp.arange(ls, dtype=F32)
    zero_state = jnp.zeros((bp, b_heads, HEAD_DIM, HEAD_DIM), F32)
    pool_halo = jnp.pad(state_pool, ((0, 0), (0, 0), (POOL_HALO - pool_ctx, 0), (0, 0)))
    cache_k = jnp.transpose(cache_band_k, (0, 1, 3, 2, 4))
    cache_v = jnp.transpose(cache_band_v, (0, 1, 3, 2, 4))
    norm_mix, norm_ffn, ret_norm, pool_scale = (
        g.reshape(depth, 1, -1) for g in (norm_mix, norm_ffn, ret_norm, pool_scale))

    def tail(z, b, rows, c0, c1):
        return lax.slice(z, ((b + 1) * lp - rows, c0), ((b + 1) * lp, c1))

    kp_l, vp_l, rp_l, pp_l, ks_l, vs_l, rs_l, ps_l = [], [], [], [], [], [], [], []
    for l in range(depth):
        if l == 0:
            h, xn = _stage([x_prompt.reshape(tp, d), x_sample.reshape(ts, d)], norm_mix, l)
        else:
            xn, = _stage([h], norm_mix, l)
        z = _wide_matmul(xn, w_in, l, relu2=False, out_dtype=F32)

        a_o = _attn_prompt(z, rel_bias_table, l, batch=bp, seq=lp, heads=a_heads, t_total=t_total)
        a_o = _attn_sample(z, cache_k, cache_v, rel_bias_table, a_o, l,
                           row0=tp, batch=bs, s_len=ls, heads=a_heads)
        b_o, s_fin = _retention(z, zero_state, ret_norm, pos_p, None, l, row0=0, batch=bp, seq=lp,
                                cb=RET_BLOCK, hp=1, heads=b_heads, col0=bq_col, t_total=t_total)
        b_o, s_new = _retention(z, state_retention, ret_norm, pos_s, b_o, l, row0=tp, batch=bs, seq=ls,
                                cb=ls, hp=b_heads, heads=b_heads, col0=bq_col, t_total=t_total)
        c_o = _pool(z, None, pool_w, pool_scale, None, l, row0=0, batch=bp, seq=lp,
                    tr=_pick_tile(lp, POOL_ROWS, POOL_HALO), pos0=0, col0=cu_col, t_total=t_total)
        c_o = _pool(z, pool_halo, pool_w, pool_scale, c_o, l, row0=tp, batch=bs, seq=ls,
                    tr=ts, pos0=PAST_LEN, col0=cu_col, t_total=t_total)

        h, hg, ssq = _outproj(h, a_o, b_o, c_o, w_out, norm_ffn, l)
        act = _wide_matmul(hg, w_up, l, relu2=True, out_dtype=BF16)
        h = _down_residual(h, ssq, act, w_down, l)

        kp_l.append(jnp.stack([tail(z, b, keep, a_width, 2 * a_width) for b in range(bp)])
                    .reshape(bp, keep, a_heads, HEAD_DIM))
        vp_l.append(jnp.stack([tail(z, b, keep, 2 * a_width, 3 * a_width) for b in range(bp)])
                    .reshape(bp, keep, a_heads, HEAD_DIM))
        rp_l.append(s_fin)
        pp_l.append(jnp.stack([tail(z, b, pool_ctx, cu0, z.shape[1]) for b in range(bp)]))
        zs_kv = lax.slice(z, (tp, a_width), (t_total, 3 * a_width)).reshape(bs, ls, 2, a_heads, HEAD_DIM)
        ks_l.append(zs_kv[:, :, 0])
        vs_l.append(zs_kv[:, :, 1])
        rs_l.append(s_new)
        cu_s = lax.slice(z, (tp, cu0), (t_total, z.shape[1])).reshape(bs, ls, -1)
        ps_l.append(jnp.concatenate([state_pool[l], cu_s], axis=1)[:, -pool_ctx:])

    y_prompt = _rmsnorm(h, norm_final, row0=0, rows=tp)
    y_sample = _rmsnorm(h, norm_final, row0=tp, rows=ts)
    return (y_prompt.reshape(bp, lp, d), y_sample.reshape(bs, ls, d),
            jnp.stack(kp_l), jnp.stack(vp_l), jnp.stack(rp_l), jnp.stack(pp_l),
            jnp.stack(ks_l), jnp.stack(vs_l), jnp.stack(rs_l), jnp.stack(ps_l))
```

```python
import functools
import math

import jax
import jax.numpy as jnp
from jax import lax
from jax.experimental import pallas as pl
from jax.experimental.pallas import tpu as pltpu

F32 = jnp.float32
BF16 = jnp.bfloat16

PAST_LEN = 1024
CHUNK = 64
BAND_CHUNKS = 8
BAND = BAND_CHUNKS * CHUNK
HEAD_DIM = 128
REL_CLIP = 128
TABLE = 2 * REL_CLIP + 1
POOL_WINDOWS = (2, 4, 8, 16)
POOL_HALO = 16
ROPE_BASE = 10000.0
EPS = 1e-6
NEG_INF = -1e30
LOG2_E = 1.4426950408889634

LANES = 128
SUBLANES_BF16 = 16
MXU_COLS = 256
VMEM_LIMIT_BYTES = 60 * 1024 * 1024

ATTN_TQ = 256
RET_BLOCK = 256
ATTN_UNROLL = 7
RET_UNROLL = 4
STAGE_ROWS = 256
NORM_ROWS = 64
ROW_TILE = 1100
WIDE_ROW_TILE = 2200
POOL_ROWS = 2048


def _pick_tile(n, target, align):
    best = None
    for t in range(align, min(n, target) + 1, align):
        if n % t == 0:
            best = t
    if best is None:
        raise ValueError(f"no tile for {n} (align {align}, target {target})")
    return best


def _params(*sem):
    return pltpu.CompilerParams(dimension_semantics=sem, vmem_limit_bytes=VMEM_LIMIT_BYTES)


def _fold_lanes(y):
    return functools.reduce(jnp.add, [y[:, c:c + LANES] for c in range(0, y.shape[1], LANES)])


def _stage_kernel(*refs, n_first, copy):
    n_out = 2 if copy else 1
    x_refs, g_ref, outs = refs[:-n_out - 1], refs[-n_out - 1], refs[-n_out:]

    def emit(x_ref):
        def body(c, carry):
            rows = pl.ds(pl.multiple_of(c * NORM_ROWS, NORM_ROWS), NORM_ROWS)
            x = x_ref[rows, :]
            if copy:
                outs[0][rows, :] = x
            ms = jnp.mean(x * x, axis=-1, keepdims=True)
            outs[-1][rows, :] = (x * lax.rsqrt(ms + EPS) * g_ref[...]).astype(BF16)
            return carry
        lax.fori_loop(0, x_ref.shape[0] // NORM_ROWS, body, 0)

    if len(x_refs) == 1:
        emit(x_refs[0])
    else:
        @pl.when(pl.program_id(0) < n_first)
        def _():
            emit(x_refs[0])

        @pl.when(pl.program_id(0) >= n_first)
        def _():
            emit(x_refs[1])


def _stage(xs, g, layer):
    d = xs[0].shape[1]
    tm = _pick_tile(math.gcd(*[x.shape[0] for x in xs]), STAGE_ROWS, NORM_ROWS)
    n_first = xs[0].shape[0] // tm
    t = sum(x.shape[0] for x in xs)
    copy = len(xs) > 1
    in_specs = [pl.BlockSpec((tm, d), lambda i: (jnp.minimum(i, n_first - 1), 0))]
    if copy:
        in_specs.append(pl.BlockSpec((tm, d), lambda i: (jnp.maximum(i - n_first, 0), 0)))
    in_specs.append(pl.BlockSpec((None, 1, d), lambda i: (layer, 0, 0)))
    out_specs = [pl.BlockSpec((tm, d), lambda i: (i, 0))]
    out_shape = [jax.ShapeDtypeStruct((t, d), BF16)]
    if copy:
        out_specs.insert(0, pl.BlockSpec((tm, d), lambda i: (i, 0)))
        out_shape.insert(0, jax.ShapeDtypeStruct((t, d), F32))
    return pl.pallas_call(
        functools.partial(_stage_kernel, n_first=n_first, copy=copy),
        grid=(t // tm,),
        in_specs=in_specs,
        out_specs=out_specs,
        out_shape=out_shape,
        compiler_params=_params("parallel"),
        name="stage_rows",
    )(*xs, g)


def _matmul_kernel(x_ref, w_ref, *rest, relu2):
    o_ref = rest[-1] if len(rest) == 1 else rest[1]
    if len(rest) == 3:
        side_in, _, side_out = rest
        side_out[...] = side_in[...].astype(BF16)
    for c in range(0, o_ref.shape[1], MXU_COLS):
        y = jnp.dot(x_ref[...], w_ref[:, c:c + MXU_COLS].astype(BF16), preferred_element_type=F32)
        if relu2:
            y = jnp.square(jnp.maximum(y, 0.0))
        o_ref[:, c:c + MXU_COLS] = y.astype(o_ref.dtype)


def _wide_matmul(x, w, layer, *, relu2, out_dtype, round_next=None):
    t, k = x.shape
    n = w.shape[2]
    tm = _pick_tile(t, WIDE_ROW_TILE, SUBLANES_BF16)
    tn = _pick_tile(n, 512, MXU_COLS)
    nj = n // tn
    in_specs = [pl.BlockSpec((tm, k), lambda i, j: (i, 0), pipeline_mode=pl.Buffered(1)),
                pl.BlockSpec((None, k, tn), lambda i, j: (layer, 0, j))]
    out_specs = [pl.BlockSpec((tm, tn), lambda i, j: (i, j))]
    out_shape = [jax.ShapeDtypeStruct((t, n), out_dtype)]
    args = [x, w]
    if round_next is not None:
        rows, cols = round_next.shape[1:]
        slab = rows // ((t // tm) * nj)
        assert slab * (t // tm) * nj == rows and slab % SUBLANES_BF16 == 0
        in_specs.append(pl.BlockSpec((None, slab, cols), lambda i, j: (layer, i * nj + j, 0)))
        out_specs.append(pl.BlockSpec((slab, cols), lambda i, j: (i * nj + j, 0)))
        out_shape.append(jax.ShapeDtypeStruct((rows, cols), BF16))
        args.append(round_next)
    outs = pl.pallas_call(
        functools.partial(_matmul_kernel, relu2=relu2),
        grid=(t // tm, nj),
        in_specs=in_specs,
        out_specs=out_specs,
        out_shape=out_shape,
        compiler_params=_params("parallel", "parallel"),
        name="matmul_relu2" if relu2 else "matmul",
    )(*args)
    return outs[0] if round_next is None else outs


def _outproj_kernel(a_ref, b_ref, c_ref, wa_ref, wb_ref, wc_ref, h_ref, g_ref, o_ref, hg_ref, ssq_ref):
    ssq = None
    for c in range(0, o_ref.shape[1], MXU_COLS):
        cols = slice(c, c + MXU_COLS)
        acc = jnp.dot(a_ref[...], wa_ref[:, cols].astype(BF16), preferred_element_type=F32)
        acc += jnp.dot(b_ref[...], wb_ref[:, cols].astype(BF16), preferred_element_type=F32)
        acc += jnp.dot(c_ref[...], wc_ref[:, cols].astype(BF16), preferred_element_type=F32)
        hn = h_ref[:, cols] + acc
        o_ref[:, cols] = hn
        hg_ref[:, cols] = (hn * g_ref[:, cols]).astype(BF16)
        part = _fold_lanes(hn * hn)
        ssq = part if ssq is None else ssq + part

    @pl.when(pl.program_id(1) == 0)
    def _():
        ssq_ref[...] = ssq

    @pl.when(pl.program_id(1) > 0)
    def _():
        ssq_ref[...] += ssq


def _outproj(h, a_o, b_o, c_o, w_out, g, layer):
    t, d = h.shape
    wa, wb, wc = a_o.shape[1], b_o.shape[1], c_o.shape[1]
    assert wa == wb and (wa + wb) % wc == 0
    tm = _pick_tile(t, ROW_TILE, SUBLANES_BF16)
    tn = _pick_tile(d, 512, MXU_COLS)
    return pl.pallas_call(
        _outproj_kernel,
        grid=(t // tm, d // tn),
        in_specs=[
            pl.BlockSpec((tm, wa), lambda i, j: (i, 0)),
            pl.BlockSpec((tm, wb), lambda i, j: (i, 0)),
            pl.BlockSpec((tm, wc), lambda i, j: (i, 0)),
            pl.BlockSpec((None, wa, tn), lambda i, j: (layer, 0, j)),
            pl.BlockSpec((None, wb, tn), lambda i, j: (layer, 1, j)),
            pl.BlockSpec((None, wc, tn), lambda i, j: (layer, (wa + wb) // wc, j)),
            pl.BlockSpec((tm, tn), lambda i, j: (i, j)),
            pl.BlockSpec((None, 1, tn), lambda i, j: (layer, 0, j)),
        ],
        out_specs=[pl.BlockSpec((tm, tn), lambda i, j: (i, j)),
                   pl.BlockSpec((tm, tn), lambda i, j: (i, j)),
                   pl.BlockSpec((tm, LANES), lambda i, j: (i, 0))],
        out_shape=[jax.ShapeDtypeStruct((t, d), F32), jax.ShapeDtypeStruct((t, d), BF16),
                   jax.ShapeDtypeStruct((t, LANES), F32)],
        compiler_params=_params("parallel", "arbitrary"),
        name="outproj_residual",
    )(a_o, b_o, c_o, w_out, w_out, w_out, h, g)


def _down_kernel(a_ref, w_ref, h_ref, ssq_ref, o_ref, *, nk, norm_dim):
    k = pl.program_id(2)
    col_chunks = [slice(c, c + MXU_COLS) for c in range(0, o_ref.shape[1], MXU_COLS)]

    def partial_product(cols):
        return jnp.dot(a_ref[...], w_ref[:, cols], preferred_element_type=F32)

    @pl.when(k == 0)
    def _():
        for cols in col_chunks:
            o_ref[:, cols] = partial_product(cols)

    @pl.when((k > 0) & (k < nk - 1))
    def _():
        for cols in col_chunks:
            o_ref[:, cols] += partial_product(cols)

    @pl.when(k == nk - 1)
    def _():
        ms = jnp.sum(ssq_ref[...], axis=-1, keepdims=True) * (1.0 / norm_dim)
        r2 = 1.0 / (ms + EPS)
        for cols in col_chunks:
            o_ref[:, cols] = h_ref[:, cols] + (o_ref[:, cols] + partial_product(cols)) * r2


def _down_residual(h, ssq, a, w):
    assert w.dtype == BF16 and a.dtype == BF16
    t, d = h.shape
    f = a.shape[1]
    tm = _pick_tile(t, WIDE_ROW_TILE, SUBLANES_BF16)
    tn = _pick_tile(d, 1024, MXU_COLS)
    tk = _pick_tile(f, 1024, LANES)
    assert f // tk >= 2
    return pl.pallas_call(
        functools.partial(_down_kernel, nk=f // tk, norm_dim=d),
        grid=(t // tm, d // tn, f // tk),
        in_specs=[
            pl.BlockSpec((tm, tk), lambda i, j, k: (i, k)),
            pl.BlockSpec((tk, tn), lambda i, j, k: (k, j)),
            pl.BlockSpec((tm, tn), lambda i, j, k: (i, j)),
            pl.BlockSpec((tm, LANES), lambda i, j, k: (i, 0), pipeline_mode=pl.Buffered(1)),
        ],
        out_specs=pl.BlockSpec((tm, tn), lambda i, j, k: (i, j)),
        out_shape=jax.ShapeDtypeStruct((t, d), F32),
        compiler_params=_params("parallel", "parallel", "arbitrary"),
        name="down_residual",
    )(a, w, h, ssq)


def _rmsnorm_kernel(x_ref, g_ref, o_ref):
    x = x_ref[...]
    ms = jnp.mean(x * x, axis=-1, keepdims=True)
    o_ref[...] = x * lax.rsqrt(ms + EPS) * g_ref[...]


def _rmsnorm(x, g, *, row0, rows):
    d = x.shape[1]
    tm = _pick_tile(rows, 256, 8)
    assert row0 % tm == 0
    return pl.pallas_call(
        _rmsnorm_kernel,
        grid=(rows // tm,),
        in_specs=[pl.BlockSpec((tm, d), lambda i: (row0 // tm + i, 0)), pl.BlockSpec((1, d), lambda i: (0, 0))],
        out_specs=pl.BlockSpec((tm, d), lambda i: (i, 0)),
        out_shape=jax.ShapeDtypeStruct((rows, d), F32),
        compiler_params=_params("parallel"),
        name="final_rmsnorm",
    )(x, g.reshape(1, d))


def _toeplitz_bias(tbl_ref, base, rows, width):
    m = lax.broadcasted_iota(jnp.int32, (8, width), 1)
    idx = jnp.where(m >= BAND + rows, 2 * REL_CLIP, jnp.clip(BAND + REL_CLIP - m, 0, 2 * REL_CLIP))

    def body(d, g):
        return jnp.where(idx == d, tbl_ref[base + d], g)

    g = lax.fori_loop(0, TABLE, body, jnp.zeros((8, width), F32))
    full = jnp.broadcast_to(g[0:1, :], (rows, width))
    return pltpu.roll(full, 0, 1, stride=1, stride_axis=0)


def _nt_dot(a, b):
    return lax.dot_general(a, b, (((1,), (1,)), ((), ())), preferred_element_type=F32)


def _attn_prompt_kernel(tbl_ref, q_ref, k_ref, v_ref, *rest, layer, heads, tq, nkb, chained):
    o_ref, kt_ref, vt_ref, bias_ref, kb_ref, vb_ref = rest[2:] if chained else rest
    head, b = pl.program_id(0), pl.program_id(1)
    seq = q_ref.shape[0]
    tk = nkb * tq
    keep = kt_ref.shape[0]
    kt_ref[...] = k_ref[seq - keep:seq, :]
    vt_ref[...] = v_ref[seq - keep:seq, :]

    @pl.when(b == 0)
    def _():
        bias = _toeplitz_bias(tbl_ref, (layer * heads + head) * TABLE, tq, tk + tq)[:, :tk]
        qc = lax.broadcasted_iota(jnp.int32, (tq, tk), 0) // CHUNK
        kc = lax.broadcasted_iota(jnp.int32, (tq, tk), 1) // CHUNK
        bias_ref[...] = jnp.where((kc >= qc) & (kc <= qc + BAND_CHUNKS), bias * LOG2_E, NEG_INF)

    kb_ref[...] = k_ref[...].astype(BF16)
    vb_ref[:, 0:HEAD_DIM] = v_ref[...].astype(BF16)
    vb_ref[:, HEAD_DIM:2 * HEAD_DIM] = jnp.ones((seq, HEAD_DIM), BF16)
    scale = HEAD_DIM ** -0.5 * LOG2_E

    def tile(q0, k0, nk, bias):
        q = (q_ref[pl.ds(q0, tq), :] * scale).astype(BF16)
        s = _nt_dot(q, kb_ref[pl.ds(k0, nk), :]) + bias
        m = jnp.max(s, axis=-1, keepdims=True)
        p = jnp.exp2(s - m)
        ol = jnp.dot(p.astype(BF16), vb_ref[pl.ds(k0, nk), :], preferred_element_type=F32)
        o_ref[pl.ds(q0, tq), :] = (ol[:, 0:HEAD_DIM] / ol[:, HEAD_DIM:2 * HEAD_DIM]).astype(o_ref.dtype)

    for i in range(nkb - 1):
        nk = (i + 1) * tq
        tile(i * tq, 0, nk, bias_ref[:, tk - nk:tk])

    def body(i, carry):
        q0 = pl.multiple_of(i * tq, tq)
        k0 = pl.multiple_of((i - (nkb - 1)) * tq, tq)
        tile(q0, k0, tk, bias_ref[...])
        return carry

    lax.fori_loop(nkb - 1, seq // tq, body, 0, unroll=ATTN_UNROLL)


def _chain(prev_arrays, first_output):
    if prev_arrays is None:
        return [], [], {}
    n = len(prev_arrays)
    return [pl.BlockSpec(memory_space=pl.ANY)] * n, list(prev_arrays), {i: first_output + i for i in range(n)}


def _attn_prompt(z, table, tails, layer, *, depth, batch, seq, keep, heads, t_total):
    tq = ATTN_TQ
    assert seq % tq == 0 and BAND % tq == 0 and tq % CHUNK == 0 and seq // tq >= BAND // tq + 1
    nkb = BAND // tq + 1
    width = heads * HEAD_DIM
    spec = lambda c: pl.BlockSpec((seq, HEAD_DIM), lambda h, b: (b, c + h))
    tail_spec = pl.BlockSpec((None, None, None, keep, HEAD_DIM), lambda h, b: (layer, b, h, 0, 0))
    tail_shape = jax.ShapeDtypeStruct((depth, batch, heads, keep, HEAD_DIM), F32)
    chain_specs, chain_args, chain_aliases = _chain(tails, 1)
    in_specs = [pl.BlockSpec(memory_space=pltpu.SMEM), spec(0), spec(heads), spec(2 * heads)]
    return pl.pallas_call(
        functools.partial(_attn_prompt_kernel, layer=layer, heads=heads, tq=tq, nkb=nkb, chained=tails is not None),
        grid=(heads, batch),
        in_specs=in_specs + chain_specs,
        out_specs=[pl.BlockSpec((seq, HEAD_DIM), lambda h, b: (b, h)), tail_spec, tail_spec],
        out_shape=[jax.ShapeDtypeStruct((t_total, width), BF16), tail_shape, tail_shape],
        scratch_shapes=[pltpu.VMEM((tq, nkb * tq), F32), pltpu.VMEM((seq, HEAD_DIM), BF16),
                        pltpu.VMEM((seq, 2 * HEAD_DIM), BF16)],
        input_output_aliases={len(in_specs) + i: o for i, o in chain_aliases.items()},
        compiler_params=_params("arbitrary", "arbitrary"),
        name="band_attention_prompt",
    )(table.reshape(-1), z, z, z, *chain_args)


def _attn_sample_kernel(tbl_ref, q_ref, kn_ref, vn_ref, kc_ref, vc_ref, *rest, layer, heads, chained):
    o_ref, ks_ref, vs_ref, bias_ref = rest[3:] if chained else rest[1:]
    s_len = q_ref.shape[0]
    la = kc_ref.shape[1]
    for h in range(heads):
        ks_ref[h] = kn_ref[:, h * HEAD_DIM:(h + 1) * HEAD_DIM]
        vs_ref[h] = vn_ref[:, h * HEAD_DIM:(h + 1) * HEAD_DIM]

    @pl.when(pl.program_id(0) == 0)
    def _():
        for h in range(heads):
            bias_ref[h] = _toeplitz_bias(tbl_ref, (layer * heads + h) * TABLE, s_len, bias_ref.shape[2])

    scale = HEAD_DIM ** -0.5
    hs = range(heads)
    sl = [slice(h * HEAD_DIM, (h + 1) * HEAD_DIM) for h in hs]
    q = [q_ref[:, sl[h]].astype(BF16) for h in hs]
    sc = [_nt_dot(q[h], kc_ref[h].astype(BF16)) * scale + bias_ref[h, :, 0:la] for h in hs]
    sn = [_nt_dot(q[h], kn_ref[:, sl[h]].astype(BF16)) * scale + bias_ref[h, :, la:la + s_len] for h in hs]
    m = [jnp.maximum(jnp.max(sc[h], axis=-1, keepdims=True), jnp.max(sn[h], axis=-1, keepdims=True)) for h in hs]
    pc = [jnp.exp(sc[h] - m[h]) for h in hs]
    pn = [jnp.exp(sn[h] - m[h]) for h in hs]
    l = [jnp.sum(pc[h], axis=-1, keepdims=True) + jnp.sum(pn[h], axis=-1, keepdims=True) for h in hs]
    o = [jnp.dot(pc[h].astype(BF16), vc_ref[h].astype(BF16), preferred_element_type=F32)
         + jnp.dot(pn[h].astype(BF16), vn_ref[:, sl[h]].astype(BF16), preferred_element_type=F32) for h in hs]
    for h in hs:
        o_ref[:, sl[h]] = (o[h] / l[h]).astype(o_ref.dtype)


def _attn_sample(z, cache_k, cache_v, table, prev, news, layer, *, row0, batch, s_len, heads):
    depth, la = cache_k.shape[0], cache_k.shape[3]
    assert la == BAND, "the relative-position bias tile assumes a full band of cached rows"
    width = heads * HEAD_DIM
    rb0 = row0 // s_len
    bias_w = pl.cdiv(la + 2 * s_len, LANES) * LANES
    new_spec = lambda c: pl.BlockSpec((s_len, width), lambda b: (rb0 + b, c))
    cache_spec = pl.BlockSpec((None, None, heads, la, HEAD_DIM), lambda b: (layer, b, 0, 0, 0))
    rows_spec = pl.BlockSpec((None, None, heads, s_len, HEAD_DIM), lambda b: (layer, b, 0, 0, 0))
    rows_shape = jax.ShapeDtypeStruct((depth, batch, heads, s_len, HEAD_DIM), F32)
    chain_specs, chain_args, chain_aliases = _chain(news, 1)
    in_specs = [pl.BlockSpec(memory_space=pltpu.SMEM), new_spec(0), new_spec(1), new_spec(2),
                cache_spec, cache_spec, pl.BlockSpec(memory_space=pl.ANY)]
    aliases = {len(in_specs) - 1: 0}
    aliases.update({len(in_specs) + i: o for i, o in chain_aliases.items()})
    return pl.pallas_call(
        functools.partial(_attn_sample_kernel, layer=layer, heads=heads, chained=news is not None),
        grid=(batch,),
        in_specs=in_specs + chain_specs,
        out_specs=[pl.BlockSpec((s_len, width), lambda b: (rb0 + b, 0)), rows_spec, rows_spec],
        out_shape=[jax.ShapeDtypeStruct(prev.shape, prev.dtype), rows_shape, rows_shape],
        scratch_shapes=[pltpu.VMEM((heads, s_len, bias_w), F32)],
        input_output_aliases=aliases,
        compiler_params=_params("arbitrary"),
        name="band_attention_sample",
    )(table.reshape(-1), z, z, z, cache_k, cache_v, prev, *chain_args)


def _ret_kernel(lg_ref, gb_ref, q_ref, k_ref, v_ref, gate_ref, cos_ref, sin_ref, gain_ref, s0_ref, *rest,
                cb, nblk, hp, n_aliased):
    rest = rest[n_aliased:]
    o_ref, sfin_ref = rest[0], rest[1]
    hgroup = pl.program_id(1)
    row = lax.broadcasted_iota(jnp.int32, (cb, cb), 0)
    col = lax.broadcasted_iota(jnp.int32, (cb, cb), 1)
    diff = (row - col).astype(F32)
    t = lax.broadcasted_iota(jnp.int32, (cb, HEAD_DIM), 0).astype(F32)

    for hh in range(hp):
        head = hgroup * hp + hh
        log_g = lg_ref[head]
        g_block = gb_ref[head]
        sl = slice(hh * HEAD_DIM, (hh + 1) * HEAD_DIM)
        key_scale = HEAD_DIM ** -0.5
        decay = jnp.where(diff >= 0, jnp.exp(jnp.maximum(diff, 0.0) * log_g) * key_scale, 0.0)
        if nblk > 1:
            decay_ref = rest[2]
            decay_ref[...] = decay
        q_w = jnp.exp((t + 1.0) * log_g)
        k_w = jnp.exp((cb - 1.0 - t) * log_g) * key_scale
        gain = gain_ref[:, sl]

        def block(n, state, sl=sl, q_w=q_w, k_w=k_w, gain=gain, decay=decay):
            r = pl.multiple_of(n * cb, cb)
            rows = pl.ds(r, cb)
            cos, sin = cos_ref[rows, :], sin_ref[rows, :]

            def rot(x):
                return x * cos + pltpu.roll(x, HEAD_DIM // 2, 1) * sin

            q = rot(q_ref[rows, sl])
            k = rot(k_ref[rows, sl])
            v = v_ref[rows, sl].astype(BF16)
            qb = q.astype(BF16)
            dec = rest[2][...] if nblk > 1 else decay
            scores = _nt_dot(qb, k.astype(BF16)) * dec
            o = jnp.dot(scores.astype(BF16), v, preferred_element_type=F32)
            o += jnp.dot(qb, state.astype(BF16), preferred_element_type=F32) * q_w
            kv = lax.dot_general((k * k_w).astype(BF16), v, (((0,), (0,)), ((), ())), preferred_element_type=F32)
            mu = jnp.mean(o, axis=-1, keepdims=True)
            var = jnp.mean(jnp.square(o - mu), axis=-1, keepdims=True)
            on = (o - mu) * lax.rsqrt(var + EPS) * gain
            gate = gate_ref[rows, sl]
            o_ref[rows, sl] = (gate * jax.nn.sigmoid(gate) * on).astype(o_ref.dtype)
            return state * g_block + kv

        state = s0_ref[hh]
        if nblk > 1:
            state = lax.fori_loop(0, nblk, block, state, unroll=RET_UNROLL)
        else:
            state = block(0, state)
        sfin_ref[hh] = state


def _retention(z, s0, gain, pos, prev, finals, layer, *, depth, row0, batch, seq, cb, hp, heads, col0, t_total):
    nblk = seq // cb
    assert seq % cb == 0 and heads % hp == 0 and row0 % seq == 0 and col0 % hp == 0
    rb0 = row0 // seq
    width = heads * HEAD_DIM
    hidx = jnp.arange(heads, dtype=F32)
    log_g = jnp.log1p(-jnp.exp2(-5.0 - hidx))
    g_block = jnp.exp(cb * log_g)
    half = HEAD_DIM // 2
    inv = ROPE_BASE ** (-jnp.arange(half, dtype=F32) / half)
    ang = pos[:, None] * inv[None, :]
    cos = jnp.concatenate([jnp.cos(ang), jnp.cos(ang)], axis=-1)
    sin = jnp.concatenate([-jnp.sin(ang), jnp.sin(ang)], axis=-1)

    bw = hp * HEAD_DIM
    zspec = lambda c: pl.BlockSpec((seq, bw), lambda b, h: (rb0 + b, (c + h * hp) // hp))
    rope_spec = pl.BlockSpec((seq, HEAD_DIM), lambda b, h: (0, 0))
    smem = pl.BlockSpec(memory_space=pltpu.SMEM)
    if s0.ndim == 5:
        s0_spec = pl.BlockSpec((None, None, hp, HEAD_DIM, HEAD_DIM), lambda b, h: (layer, b, h, 0, 0))
    else:
        s0_spec = pl.BlockSpec((None, hp, HEAD_DIM, HEAD_DIM), lambda b, h: (b, h, 0, 0))
    gain_spec = pl.BlockSpec((None, 1, bw), lambda b, h: (layer, 0, h))
    in_specs = [smem, smem, zspec(col0), zspec(col0 + heads), zspec(col0 + 2 * heads), zspec(col0 + 3 * heads),
                rope_spec, rope_spec, gain_spec, s0_spec]
    args = [log_g, g_block, z, z, z, z, cos, sin, gain, s0]
    aliases = {}
    for out_index, arr in enumerate((prev, finals)):
        if arr is not None:
            in_specs.append(pl.BlockSpec(memory_space=pl.ANY))
            args.append(arr)
            aliases[len(args) - 1] = out_index
    return pl.pallas_call(
        functools.partial(_ret_kernel, cb=cb, nblk=nblk, hp=hp, n_aliased=len(aliases)),
        grid=(batch, heads // hp),
        in_specs=in_specs,
        out_specs=[pl.BlockSpec((seq, bw), lambda b, h: (rb0 + b, h)),
                   pl.BlockSpec((None, None, hp, HEAD_DIM, HEAD_DIM), lambda b, h: (layer, b, h, 0, 0))],
        out_shape=[jax.ShapeDtypeStruct((t_total, width), BF16),
                   jax.ShapeDtypeStruct((depth, batch, heads, HEAD_DIM, HEAD_DIM), F32)],
        scratch_shapes=[pltpu.VMEM((cb, cb), F32)] if nblk > 1 else [],
        input_output_aliases=aliases,
        compiler_params=_params("arbitrary", "arbitrary"),
        name="retention_prompt" if prev is None else "retention_sample",
    )(*args)


def _pool_kernel(u_ref, halo_ref, w_ref, sc_ref, *rest, pos0, nseq, first_tile_has_no_history, aliased):
    if aliased:
        rest = rest[1:]
    o_ref, x_ref = rest
    i, g = pl.program_id(1), pl.program_id(2)
    tr = u_ref.shape[0]
    sr = tr // nseq
    for s in range(nseq):
        halo = halo_ref[s] if nseq > 1 else halo_ref[...]
        if first_tile_has_no_history:
            halo = jnp.where(i == 0, 0.0, halo)
        x_ref[s, 0:POOL_HALO, :] = halo
        x_ref[s, POOL_HALO:POOL_HALO + sr, :] = u_ref[s * sr:(s + 1) * sr, :]
    pos = (pos0 + i * sr + lax.broadcasted_iota(jnp.int32, (sr, 1), 0)).astype(F32)

    for gi, w in enumerate(POOL_WINDOWS):
        @pl.when(g == gi)
        def _(w=w):
            inv_cnt = 1.0 / jnp.minimum(float(w), pos + 1.0)
            parts = []
            for s in range(nseq):
                win = x_ref[s, POOL_HALO:POOL_HALO + sr, :]
                for j in range(1, w):
                    win = win + x_ref[s, POOL_HALO - j:POOL_HALO - j + sr, :]
                parts.append(win * inv_cnt - x_ref[s, POOL_HALO:POOL_HALO + sr, :])
            diff = parts[0] if nseq == 1 else jnp.concatenate(parts, axis=0)
            y = jnp.dot(diff.astype(BF16), w_ref[...].astype(BF16), preferred_element_type=F32)
            o_ref[...] = (y * sc_ref[...]).astype(o_ref.dtype)


def _pool(z, halo, w_pool, scale, prev, layer, *, row0, batch, seq, tr, pos0, col0, t_total):
    groups, gw = w_pool.shape[1], w_pool.shape[2]
    assert groups == len(POOL_WINDOWS) and max(POOL_WINDOWS) - 1 <= POOL_HALO and tr % POOL_HALO == 0
    in_specs = []
    if halo is None:
        nseq, nt, nb = 1, seq // tr, batch
        rb0 = row0 // tr
        per = tr // POOL_HALO
        in_specs += [pl.BlockSpec((tr, gw), lambda b, i, g: (rb0 + b * nt + i, col0 + g)),
                     pl.BlockSpec((POOL_HALO, gw),
                                  lambda b, i, g: (jnp.maximum((rb0 + b * nt + i) * per - 1, 0), col0 + g))]
        args = [z, z]
    else:
        assert tr == batch * seq and row0 % tr == 0
        nseq, nt, nb = batch, 1, 1
        rb0 = row0 // tr
        in_specs += [pl.BlockSpec((tr, gw), lambda b, i, g: (rb0, col0 + g)),
                     pl.BlockSpec((None, batch, POOL_HALO, gw), lambda b, i, g: (layer, 0, 0, g))]
        args = [z, halo]
    in_specs += [pl.BlockSpec((None, None, gw, gw), lambda b, i, g: (layer, g, 0, 0)),
                 pl.BlockSpec((None, 1, gw), lambda b, i, g: (layer, 0, g))]
    args += [w_pool, scale]
    aliases = {}
    if prev is not None:
        in_specs.append(pl.BlockSpec(memory_space=pl.ANY))
        args.append(prev)
        aliases = {len(args) - 1: 0}
    return pl.pallas_call(
        functools.partial(_pool_kernel, pos0=pos0, nseq=nseq, first_tile_has_no_history=halo is None,
                          aliased=prev is not None),
        grid=(nb, nt, groups),
        in_specs=in_specs,
        out_specs=pl.BlockSpec((tr, gw), lambda b, i, g: (rb0 + b * nt + i, g)),
        out_shape=jax.ShapeDtypeStruct((t_total, groups * gw), BF16),
        scratch_shapes=[pltpu.VMEM((nseq, POOL_HALO + tr // nseq, gw), F32)],
        input_output_aliases=aliases,
        compiler_params=_params("arbitrary", "arbitrary", "arbitrary"),
        name="pool_prompt" if prev is None else "pool_sample",
    )(*args)


def kernel(x_prompt, x_sample, cache_band_k, cache_band_v, state_retention, state_pool, norm_mix, w_in,
           rel_bias_table, ret_norm, pool_w, pool_scale, w_out, norm_ffn, w_up, w_down, norm_final):
    bp, lp, d = x_prompt.shape
    bs, ls, _ = x_sample.shape
    depth = w_in.shape[0]
    a_heads = rel_bias_table.shape[1]
    a_width = a_heads * HEAD_DIM
    b_width = ret_norm.shape[1]
    b_heads = b_width // HEAD_DIM
    gw = pool_w.shape[2]
    tp, ts = bp * lp, bs * ls
    t_total = tp + ts
    keep = min(BAND, lp)
    pool_ctx = state_pool.shape[2]
    assert a_width % gw == 0 and (3 * a_width + 4 * b_width) % gw == 0 and tp % ts == 0
    bq_col = 3 * a_heads
    cu_col = (3 * a_width + 4 * b_width) // gw
    cu0 = cu_col * gw

    pos_p = jnp.arange(lp, dtype=F32)
    pos_s = PAST_LEN + jnp.arange(ls, dtype=F32)
    zero_state = jnp.zeros((bp, b_heads, HEAD_DIM, HEAD_DIM), F32)
    pool_halo = jnp.pad(state_pool, ((0, 0), (0, 0), (POOL_HALO - pool_ctx, 0), (0, 0)))
    cache_k = jnp.transpose(cache_band_k, (0, 1, 3, 2, 4))
    cache_v = jnp.transpose(cache_band_v, (0, 1, 3, 2, 4))
    norm_mix, norm_ffn, ret_norm, pool_scale = (
        g.reshape(depth, 1, -1) for g in (norm_mix, norm_ffn, ret_norm, pool_scale))

    def tail(z, b, rows, c0, c1):
        return lax.slice(z, ((b + 1) * lp - rows, c0), ((b + 1) * lp, c1))

    pp_l, ps_l = [], []
    tails = news = ret_p = ret_s = None
    for l in range(depth):
        if l == 0:
            h, xn = _stage([x_prompt.reshape(tp, d), x_sample.reshape(ts, d)], norm_mix, l)
        else:
            xn, = _stage([h], norm_mix, l)
        z = _wide_matmul(xn, w_in, l, relu2=False, out_dtype=F32)

        a_o, *tails = _attn_prompt(z, rel_bias_table, tails, l, depth=depth, batch=bp, seq=lp, keep=keep,
                                   heads=a_heads, t_total=t_total)
        a_o, *news = _attn_sample(z, cache_k, cache_v, rel_bias_table, a_o, news, l,
                                  row0=tp, batch=bs, s_len=ls, heads=a_heads)
        b_o, ret_p = _retention(z, zero_state, ret_norm, pos_p, None, ret_p, l, depth=depth, row0=0, batch=bp,
                                seq=lp, cb=RET_BLOCK, hp=1, heads=b_heads, col0=bq_col, t_total=t_total)
        b_o, ret_s = _retention(z, state_retention, ret_norm, pos_s, b_o, ret_s, l, depth=depth, row0=tp, batch=bs,
                                seq=ls, cb=ls, hp=b_heads, heads=b_heads, col0=bq_col, t_total=t_total)
        c_o = _pool(z, None, pool_w, pool_scale, None, l, row0=0, batch=bp, seq=lp,
                    tr=_pick_tile(lp, POOL_ROWS, POOL_HALO), pos0=0, col0=cu_col, t_total=t_total)
        c_o = _pool(z, pool_halo, pool_w, pool_scale, c_o, l, row0=tp, batch=bs, seq=ls,
                    tr=ts, pos0=PAST_LEN, col0=cu_col, t_total=t_total)

        h, hg, ssq = _outproj(h, a_o, b_o, c_o, w_out, norm_ffn, l)
        act, w_down_bf16 = _wide_matmul(hg, w_up, l, relu2=True, out_dtype=BF16, round_next=w_down)
        h = _down_residual(h, ssq, act, w_down_bf16)

        pp_l.append(jnp.stack([tail(z, b, pool_ctx, cu0, z.shape[1]) for b in range(bp)]))
        cu_s = lax.slice(z, (tp, cu0), (t_total, z.shape[1])).reshape(bs, ls, -1)
        ps_l.append(jnp.concatenate([state_pool[l], cu_s], axis=1)[:, -pool_ctx:])

    y_prompt = _rmsnorm(h, norm_final, row0=0, rows=tp)
    y_sample = _rmsnorm(h, norm_final, row0=tp, rows=ts)
    rows_major = lambda x: jnp.transpose(x, (0, 1, 3, 2, 4))
    return (y_prompt.reshape(bp, lp, d), y_sample.reshape(bs, ls, d),
            rows_major(tails[0]), rows_major(tails[1]), ret_p, jnp.stack(pp_l),
            rows_major(news[0]), rows_major(news[1]), ret_s, jnp.stack(ps_l))
```

```python
import functools
import math

import jax
import jax.numpy as jnp
from jax import lax
from jax.experimental import pallas as pl
from jax.experimental.pallas import tpu as pltpu

F32 = jnp.float32
BF16 = jnp.bfloat16

PAST_LEN = 1024
CHUNK = 64
BAND_CHUNKS = 8
BAND = BAND_CHUNKS * CHUNK
HEAD_DIM = 128
REL_CLIP = 128
TABLE = 2 * REL_CLIP + 1
POOL_WINDOWS = (2, 4, 8, 16)
POOL_HALO = 16
ROPE_BASE = 10000.0
EPS = 1e-6
NEG_INF = -1e30
LOG2_E = 1.4426950408889634

LANES = 128
SUBLANES_BF16 = 16
MXU_COLS = 256
VMEM_LIMIT_BYTES = 60 * 1024 * 1024

ATTN_TQ = 256
RET_BLOCK = 256
ATTN_UNROLL = 8
RET_UNROLL = 4
DOWN_K = 1024
STAGE_ROWS = 256
NORM_ROWS = 64
ROW_TILE = 1100
WIDE_ROW_TILE = 2200
POOL_ROWS = 2048


def _pick_tile(n, target, align):
    best = None
    for t in range(align, min(n, target) + 1, align):
        if n % t == 0:
            best = t
    if best is None:
        raise ValueError(f"no tile for {n} (align {align}, target {target})")
    return best


def _params(*sem):
    return pltpu.CompilerParams(dimension_semantics=sem, vmem_limit_bytes=VMEM_LIMIT_BYTES)


def _fold_lanes(y):
    return functools.reduce(jnp.add, [y[:, c:c + LANES] for c in range(0, y.shape[1], LANES)])


def _stage_kernel(*refs, n_first, copy):
    n_out = 2 if copy else 1
    x_refs, g_ref, outs = refs[:-n_out - 1], refs[-n_out - 1], refs[-n_out:]

    def emit(x_ref):
        def body(c, carry):
            rows = pl.ds(pl.multiple_of(c * NORM_ROWS, NORM_ROWS), NORM_ROWS)
            x = x_ref[rows, :]
            if copy:
                outs[0][rows, :] = x
            ms = jnp.mean(x * x, axis=-1, keepdims=True)
            outs[-1][rows, :] = (x * lax.rsqrt(ms + EPS) * g_ref[...]).astype(BF16)
            return carry
        lax.fori_loop(0, x_ref.shape[0] // NORM_ROWS, body, 0)

    if len(x_refs) == 1:
        emit(x_refs[0])
    else:
        @pl.when(pl.program_id(0) < n_first)
        def _():
            emit(x_refs[0])

        @pl.when(pl.program_id(0) >= n_first)
        def _():
            emit(x_refs[1])


def _stage(xs, g, layer):
    d = xs[0].shape[1]
    tm = _pick_tile(math.gcd(*[x.shape[0] for x in xs]), STAGE_ROWS, NORM_ROWS)
    n_first = xs[0].shape[0] // tm
    t = sum(x.shape[0] for x in xs)
    copy = len(xs) > 1
    in_specs = [pl.BlockSpec((tm, d), lambda i: (jnp.minimum(i, n_first - 1), 0))]
    if copy:
        in_specs.append(pl.BlockSpec((tm, d), lambda i: (jnp.maximum(i - n_first, 0), 0)))
    in_specs.append(pl.BlockSpec((None, 1, d), lambda i: (layer, 0, 0)))
    out_specs = [pl.BlockSpec((tm, d), lambda i: (i, 0))]
    out_shape = [jax.ShapeDtypeStruct((t, d), BF16)]
    if copy:
        out_specs.insert(0, pl.BlockSpec((tm, d), lambda i: (i, 0)))
        out_shape.insert(0, jax.ShapeDtypeStruct((t, d), F32))
    return pl.pallas_call(
        functools.partial(_stage_kernel, n_first=n_first, copy=copy),
        grid=(t // tm,),
        in_specs=in_specs,
        out_specs=out_specs,
        out_shape=out_shape,
        compiler_params=_params("parallel"),
        name="stage_rows",
    )(*xs, g)


def _matmul_kernel(x_ref, w_ref, o_ref, *, relu2):
    for c in range(0, o_ref.shape[1], MXU_COLS):
        y = jnp.dot(x_ref[...], w_ref[:, c:c + MXU_COLS].astype(BF16), preferred_element_type=F32)
        if relu2:
            y = jnp.square(jnp.maximum(y, 0.0))
        o_ref[:, c:c + MXU_COLS] = y.astype(o_ref.dtype)


def _wide_matmul(x, w, layer, *, relu2, out_dtype, panel=None):
    t, k = x.shape
    n = w.shape[2]
    tm = _pick_tile(t, WIDE_ROW_TILE, SUBLANES_BF16)
    tn = _pick_tile(n, 512, MXU_COLS)
    if panel is None:
        out_spec = pl.BlockSpec((tm, tn), lambda i, j: (i, j))
        out_shape = jax.ShapeDtypeStruct((t, n), out_dtype)
    else:
        assert panel % tn == 0 and n % panel == 0
        per = panel // tn
        out_spec = pl.BlockSpec((None, tm, tn), lambda i, j: (j // per, i, j % per))
        out_shape = jax.ShapeDtypeStruct((n // panel, t, panel), out_dtype)
    return pl.pallas_call(
        functools.partial(_matmul_kernel, relu2=relu2),
        grid=(t // tm, n // tn),
        in_specs=[pl.BlockSpec((tm, k), lambda i, j: (i, 0), pipeline_mode=pl.Buffered(1)),
                  pl.BlockSpec((None, k, tn), lambda i, j: (layer, 0, j))],
        out_specs=out_spec,
        out_shape=out_shape,
        compiler_params=_params("parallel", "parallel"),
        name="matmul_relu2" if relu2 else "matmul",
    )(x, w)


def _outproj_kernel(a_ref, b_ref, c_ref, wa_ref, wb_ref, wc_ref, h_ref, g_ref, o_ref, hg_ref, ssq_ref):
    ssq = None
    for c in range(0, o_ref.shape[1], MXU_COLS):
        cols = slice(c, c + MXU_COLS)
        acc = jnp.dot(a_ref[...], wa_ref[:, cols].astype(BF16), preferred_element_type=F32)
        acc += jnp.dot(b_ref[...], wb_ref[:, cols].astype(BF16), preferred_element_type=F32)
        acc += jnp.dot(c_ref[...], wc_ref[:, cols].astype(BF16), preferred_element_type=F32)
        hn = h_ref[:, cols] + acc
        o_ref[:, cols] = hn
        hg_ref[:, cols] = (hn * g_ref[:, cols]).astype(BF16)
        part = _fold_lanes(hn * hn)
        ssq = part if ssq is None else ssq + part

    @pl.when(pl.program_id(1) == 0)
    def _():
        ssq_ref[...] = ssq

    @pl.when(pl.program_id(1) > 0)
    def _():
        ssq_ref[...] += ssq


def _outproj(h, a_o, b_o, c_o, w_out, g, layer):
    t, d = h.shape
    wa, wb, wc = a_o.shape[1], b_o.shape[1], c_o.shape[1]
    assert wa == wb and (wa + wb) % wc == 0
    tm = _pick_tile(t, ROW_TILE, SUBLANES_BF16)
    tn = _pick_tile(d, 512, MXU_COLS)
    return pl.pallas_call(
        _outproj_kernel,
        grid=(t // tm, d // tn),
        in_specs=[
            pl.BlockSpec((tm, wa), lambda i, j: (i, 0)),
            pl.BlockSpec((tm, wb), lambda i, j: (i, 0)),
            pl.BlockSpec((tm, wc), lambda i, j: (i, 0)),
            pl.BlockSpec((None, wa, tn), lambda i, j: (layer, 0, j)),
            pl.BlockSpec((None, wb, tn), lambda i, j: (layer, 1, j)),
            pl.BlockSpec((None, wc, tn), lambda i, j: (layer, (wa + wb) // wc, j)),
            pl.BlockSpec((tm, tn), lambda i, j: (i, j)),
            pl.BlockSpec((None, 1, tn), lambda i, j: (layer, 0, j)),
        ],
        out_specs=[pl.BlockSpec((tm, tn), lambda i, j: (i, j)),
                   pl.BlockSpec((tm, tn), lambda i, j: (i, j)),
                   pl.BlockSpec((tm, LANES), lambda i, j: (i, 0))],
        out_shape=[jax.ShapeDtypeStruct((t, d), F32), jax.ShapeDtypeStruct((t, d), BF16),
                   jax.ShapeDtypeStruct((t, LANES), F32)],
        compiler_params=_params("parallel", "arbitrary"),
        name="outproj_residual",
    )(a_o, b_o, c_o, w_out, w_out, w_out, h, g)


def _down_kernel(a_ref, w_ref, h_ref, ssq_ref, o_ref, *, nk, norm_dim):
    k = pl.program_id(2)
    col_chunks = [slice(c, c + MXU_COLS) for c in range(0, o_ref.shape[1], MXU_COLS)]

    def partial_product(cols):
        return jnp.dot(a_ref[...], w_ref[:, cols].astype(BF16), preferred_element_type=F32)

    @pl.when(k == 0)
    def _():
        for cols in col_chunks:
            o_ref[:, cols] = partial_product(cols)

    @pl.when((k > 0) & (k < nk - 1))
    def _():
        for cols in col_chunks:
            o_ref[:, cols] += partial_product(cols)

    @pl.when(k == nk - 1)
    def _():
        ms = jnp.sum(ssq_ref[...], axis=-1, keepdims=True) * (1.0 / norm_dim)
        r2 = 1.0 / (ms + EPS)
        for cols in col_chunks:
            o_ref[:, cols] = h_ref[:, cols] + (o_ref[:, cols] + partial_product(cols)) * r2


def _down_residual(h, ssq, a, w, layer):
    t, d = h.shape
    nk, _, tk = a.shape
    f = nk * tk
    tm = _pick_tile(t, WIDE_ROW_TILE, SUBLANES_BF16)
    tn = _pick_tile(d, 1024, MXU_COLS)
    assert nk >= 2 and w.shape[1] == f
    return pl.pallas_call(
        functools.partial(_down_kernel, nk=nk, norm_dim=d),
        grid=(t // tm, d // tn, nk),
        in_specs=[
            pl.BlockSpec((None, tm, tk), lambda i, j, k: (k, i, 0)),
            pl.BlockSpec((None, tk, tn), lambda i, j, k: (layer, k, j)),
            pl.BlockSpec((tm, tn), lambda i, j, k: (i, j)),
            pl.BlockSpec((tm, LANES), lambda i, j, k: (i, 0), pipeline_mode=pl.Buffered(1)),
        ],
        out_specs=pl.BlockSpec((tm, tn), lambda i, j, k: (i, j)),
        out_shape=jax.ShapeDtypeStruct((t, d), F32),
        compiler_params=_params("parallel", "parallel", "arbitrary"),
        name="down_residual",
    )(a, w, h, ssq)


def _rmsnorm_kernel(x_ref, g_ref, o_ref):
    x = x_ref[...]
    ms = jnp.mean(x * x, axis=-1, keepdims=True)
    o_ref[...] = x * lax.rsqrt(ms + EPS) * g_ref[...]


def _rmsnorm(x, g, *, row0, rows):
    d = x.shape[1]
    tm = _pick_tile(rows, 256, 8)
    assert row0 % tm == 0
    return pl.pallas_call(
        _rmsnorm_kernel,
        grid=(rows // tm,),
        in_specs=[pl.BlockSpec((tm, d), lambda i: (row0 // tm + i, 0)), pl.BlockSpec((1, d), lambda i: (0, 0))],
        out_specs=pl.BlockSpec((tm, d), lambda i: (i, 0)),
        out_shape=jax.ShapeDtypeStruct((rows, d), F32),
        compiler_params=_params("parallel"),
        name="final_rmsnorm",
    )(x, g.reshape(1, d))


def _toeplitz_bias(tbl_ref, base, rows, width):
    m = lax.broadcasted_iota(jnp.int32, (8, width), 1)
    idx = jnp.where(m >= BAND + rows, 2 * REL_CLIP, jnp.clip(BAND + REL_CLIP - m, 0, 2 * REL_CLIP))

    def body(d, g):
        return jnp.where(idx == d, tbl_ref[base + d], g)

    g = lax.fori_loop(0, TABLE, body, jnp.zeros((8, width), F32))
    full = jnp.broadcast_to(g[0:1, :], (rows, width))
    return pltpu.roll(full, 0, 1, stride=1, stride_axis=0)


def _nt_dot(a, b):
    return lax.dot_general(a, b, (((1,), (1,)), ((), ())), preferred_element_type=F32)


def _attn_prompt_kernel(tbl_ref, q_ref, k_ref, v_ref, *rest, layer, heads, tq, nkb, chained):
    o_ref, kt_ref, vt_ref, bias_ref, kb_ref, vb_ref = rest[2:] if chained else rest
    head, b = pl.program_id(0), pl.program_id(1)
    seq = q_ref.shape[0]
    tk = nkb * tq
    keep = kt_ref.shape[0]
    kt_ref[...] = k_ref[seq - keep:seq, :]
    vt_ref[...] = v_ref[seq - keep:seq, :]

    @pl.when(b == 0)
    def _():
        bias = _toeplitz_bias(tbl_ref, (layer * heads + head) * TABLE, tq, tk + tq)[:, :tk]
        qc = lax.broadcasted_iota(jnp.int32, (tq, tk), 0) // CHUNK
        kc = lax.broadcasted_iota(jnp.int32, (tq, tk), 1) // CHUNK
        bias_ref[...] = jnp.where((kc >= qc) & (kc <= qc + BAND_CHUNKS), bias * LOG2_E, NEG_INF)

    kb_ref[...] = k_ref[...].astype(BF16)
    vb_ref[:, 0:HEAD_DIM] = v_ref[...].astype(BF16)
    vb_ref[:, HEAD_DIM:2 * HEAD_DIM] = jnp.ones((seq, HEAD_DIM), BF16)
    scale = HEAD_DIM ** -0.5 * LOG2_E

    def tile(q0, k0, nk, bias):
        q = (q_ref[pl.ds(q0, tq), :] * scale).astype(BF16)
        s = _nt_dot(q, kb_ref[pl.ds(k0, nk), :]) + bias
        m = jnp.max(s, axis=-1, keepdims=True)
        p = jnp.exp2(s - m)
        ol = jnp.dot(p.astype(BF16), vb_ref[pl.ds(k0, nk), :], preferred_element_type=F32)
        o_ref[pl.ds(q0, tq), :] = (ol[:, 0:HEAD_DIM] / ol[:, HEAD_DIM:2 * HEAD_DIM]).astype(o_ref.dtype)

    for i in range(nkb - 1):
        nk = (i + 1) * tq
        tile(i * tq, 0, nk, bias_ref[:, tk - nk:tk])

    def body(i, carry):
        q0 = pl.multiple_of(i * tq, tq)
        k0 = pl.multiple_of((i - (nkb - 1)) * tq, tq)
        tile(q0, k0, tk, bias_ref[...])
        return carry

    n_loop = seq // tq - (nkb - 1)
    unroll = max(u for u in range(1, ATTN_UNROLL + 1) if n_loop % u == 0)
    lax.fori_loop(nkb - 1, seq // tq, body, 0, unroll=unroll)


def _chain(prev_arrays, first_output):
    if prev_arrays is None:
        return [], [], {}
    n = len(prev_arrays)
    return [pl.BlockSpec(memory_space=pl.ANY)] * n, list(prev_arrays), {i: first_output + i for i in range(n)}


def _attn_prompt(z, table, tails, layer, *, depth, batch, seq, keep, heads, t_total):
    tq = ATTN_TQ
    assert seq % tq == 0 and BAND % tq == 0 and tq % CHUNK == 0 and seq // tq >= BAND // tq + 1
    nkb = BAND // tq + 1
    width = heads * HEAD_DIM
    spec = lambda c: pl.BlockSpec((seq, HEAD_DIM), lambda h, b: (b, c + h))
    tail_spec = pl.BlockSpec((None, None, None, keep, HEAD_DIM), lambda h, b: (layer, b, h, 0, 0))
    tail_shape = jax.ShapeDtypeStruct((depth, batch, heads, keep, HEAD_DIM), F32)
    chain_specs, chain_args, chain_aliases = _chain(tails, 1)
    in_specs = [pl.BlockSpec(memory_space=pltpu.SMEM), spec(0), spec(heads), spec(2 * heads)]
    return pl.pallas_call(
        functools.partial(_attn_prompt_kernel, layer=layer, heads=heads, tq=tq, nkb=nkb, chained=tails is not None),
        grid=(heads, batch),
        in_specs=in_specs + chain_specs,
        out_specs=[pl.BlockSpec((seq, HEAD_DIM), lambda h, b: (b, h)), tail_spec, tail_spec],
        out_shape=[jax.ShapeDtypeStruct((t_total, width), BF16), tail_shape, tail_shape],
        scratch_shapes=[pltpu.VMEM((tq, nkb * tq), F32), pltpu.VMEM((seq, HEAD_DIM), BF16),
                        pltpu.VMEM((seq, 2 * HEAD_DIM), BF16)],
        input_output_aliases={len(in_specs) + i: o for i, o in chain_aliases.items()},
        compiler_params=_params("arbitrary", "arbitrary"),
        name="band_attention_prompt",
    )(table.reshape(-1), z, z, z, *chain_args)


def _attn_sample_kernel(tbl_ref, q_ref, kn_ref, vn_ref, kc_ref, vc_ref, *rest, layer, heads, chained):
    o_ref, ks_ref, vs_ref, bias_ref = rest[3:] if chained else rest[1:]
    s_len = q_ref.shape[0]
    la = kc_ref.shape[1]
    for h in range(heads):
        ks_ref[h] = kn_ref[:, h * HEAD_DIM:(h + 1) * HEAD_DIM]
        vs_ref[h] = vn_ref[:, h * HEAD_DIM:(h + 1) * HEAD_DIM]

    @pl.when(pl.program_id(0) == 0)
    def _():
        for h in range(heads):
            bias_ref[h] = _toeplitz_bias(tbl_ref, (layer * heads + h) * TABLE, s_len, bias_ref.shape[2])

    scale = HEAD_DIM ** -0.5
    hs = range(heads)
    sl = [slice(h * HEAD_DIM, (h + 1) * HEAD_DIM) for h in hs]
    q = [q_ref[:, sl[h]].astype(BF16) for h in hs]
    sc = [_nt_dot(q[h], kc_ref[h].astype(BF16)) * scale + bias_ref[h, :, 0:la] for h in hs]
    sn = [_nt_dot(q[h], kn_ref[:, sl[h]].astype(BF16)) * scale + bias_ref[h, :, la:la + s_len] for h in hs]
    m = [jnp.maximum(jnp.max(sc[h], axis=-1, keepdims=True), jnp.max(sn[h], axis=-1, keepdims=True)) for h in hs]
    pc = [jnp.exp(sc[h] - m[h]) for h in hs]
    pn = [jnp.exp(sn[h] - m[h]) for h in hs]
    l = [jnp.sum(pc[h], axis=-1, keepdims=True) + jnp.sum(pn[h], axis=-1, keepdims=True) for h in hs]
    o = [jnp.dot(pc[h].astype(BF16), vc_ref[h].astype(BF16), preferred_element_type=F32)
         + jnp.dot(pn[h].astype(BF16), vn_ref[:, sl[h]].astype(BF16), preferred_element_type=F32) for h in hs]
    for h in hs:
        o_ref[:, sl[h]] = (o[h] / l[h]).astype(o_ref.dtype)


def _attn_sample(z, cache_k, cache_v, table, prev, news, layer, *, row0, batch, s_len, heads):
    depth, la = cache_k.shape[0], cache_k.shape[3]
    assert la == BAND, "the relative-position bias tile assumes a full band of cached rows"
    width = heads * HEAD_DIM
    rb0 = row0 // s_len
    bias_w = pl.cdiv(la + 2 * s_len, LANES) * LANES
    new_spec = lambda c: pl.BlockSpec((s_len, width), lambda b: (rb0 + b, c))
    cache_spec = pl.BlockSpec((None, None, heads, la, HEAD_DIM), lambda b: (layer, b, 0, 0, 0))
    rows_spec = pl.BlockSpec((None, None, heads, s_len, HEAD_DIM), lambda b: (layer, b, 0, 0, 0))
    rows_shape = jax.ShapeDtypeStruct((depth, batch, heads, s_len, HEAD_DIM), F32)
    chain_specs, chain_args, chain_aliases = _chain(news, 1)
    in_specs = [pl.BlockSpec(memory_space=pltpu.SMEM), new_spec(0), new_spec(1), new_spec(2),
                cache_spec, cache_spec, pl.BlockSpec(memory_space=pl.ANY)]
    aliases = {len(in_specs) - 1: 0}
    aliases.update({len(in_specs) + i: o for i, o in chain_aliases.items()})
    return pl.pallas_call(
        functools.partial(_attn_sample_kernel, layer=layer, heads=heads, chained=news is not None),
        grid=(batch,),
        in_specs=in_specs + chain_specs,
        out_specs=[pl.BlockSpec((s_len, width), lambda b: (rb0 + b, 0)), rows_spec, rows_spec],
        out_shape=[jax.ShapeDtypeStruct(prev.shape, prev.dtype), rows_shape, rows_shape],
        scratch_shapes=[pltpu.VMEM((heads, s_len, bias_w), F32)],
        input_output_aliases=aliases,
        compiler_params=_params("arbitrary"),
        name="band_attention_sample",
    )(table.reshape(-1), z, z, z, cache_k, cache_v, prev, *chain_args)


def _ret_kernel(lg_ref, gb_ref, q_ref, k_ref, v_ref, gate_ref, cos_ref, sin_ref, gain_ref, s0_ref, *rest,
                cb, nblk, hp, n_aliased):
    rest = rest[n_aliased:]
    o_ref, sfin_ref = rest[0], rest[1]
    hgroup = pl.program_id(1)
    row = lax.broadcasted_iota(jnp.int32, (cb, cb), 0)
    col = lax.broadcasted_iota(jnp.int32, (cb, cb), 1)
    diff = (row - col).astype(F32)
    t = lax.broadcasted_iota(jnp.int32, (cb, HEAD_DIM), 0).astype(F32)

    for hh in range(hp):
        head = hgroup * hp + hh
        log_g = lg_ref[head]
        g_block = gb_ref[head]
        sl = slice(hh * HEAD_DIM, (hh + 1) * HEAD_DIM)
        key_scale = HEAD_DIM ** -0.5
        decay = jnp.where(diff >= 0, jnp.exp(jnp.maximum(diff, 0.0) * log_g) * key_scale, 0.0)
        if nblk > 1:
            decay_ref = rest[2]
            decay_ref[...] = decay
        q_w = jnp.exp((t + 1.0) * log_g)
        k_w = jnp.exp((cb - 1.0 - t) * log_g) * key_scale
        gain = gain_ref[:, sl]

        def block(n, state, sl=sl, q_w=q_w, k_w=k_w, gain=gain, decay=decay):
            r = pl.multiple_of(n * cb, cb)
            rows = pl.ds(r, cb)
            cos, sin = cos_ref[rows, :], sin_ref[rows, :]

            def rot(x):
                return x * cos + pltpu.roll(x, HEAD_DIM // 2, 1) * sin

            q = rot(q_ref[rows, sl])
            k = rot(k_ref[rows, sl])
            v = v_ref[rows, sl].astype(BF16)
            qb = q.astype(BF16)
            dec = rest[2][...] if nblk > 1 else decay
            scores = _nt_dot(qb, k.astype(BF16)) * dec
            o = jnp.dot(scores.astype(BF16), v, preferred_element_type=F32)
            o += jnp.dot(qb, state.astype(BF16), preferred_element_type=F32) * q_w
            kv = lax.dot_general((k * k_w).astype(BF16), v, (((0,), (0,)), ((), ())), preferred_element_type=F32)
            mu = jnp.mean(o, axis=-1, keepdims=True)
            var = jnp.mean(jnp.square(o - mu), axis=-1, keepdims=True)
            on = (o - mu) * lax.rsqrt(var + EPS) * gain
            gate = gate_ref[rows, sl]
            o_ref[rows, sl] = (gate * jax.nn.sigmoid(gate) * on).astype(o_ref.dtype)
            return state * g_block + kv

        state = s0_ref[hh]
        if nblk > 1:
            state = lax.fori_loop(0, nblk, block, state, unroll=RET_UNROLL)
        else:
            state = block(0, state)
        sfin_ref[hh] = state


def _retention(z, s0, gain, pos, prev, finals, layer, *, depth, row0, batch, seq, cb, hp, heads, col0, t_total):
    nblk = seq // cb
    assert seq % cb == 0 and heads % hp == 0 and row0 % seq == 0 and col0 % hp == 0
    rb0 = row0 // seq
    width = heads * HEAD_DIM
    hidx = jnp.arange(heads, dtype=F32)
    log_g = jnp.log1p(-jnp.exp2(-5.0 - hidx))
    g_block = jnp.exp(cb * log_g)
    half = HEAD_DIM // 2
    inv = ROPE_BASE ** (-jnp.arange(half, dtype=F32) / half)
    ang = pos[:, None] * inv[None, :]
    cos = jnp.concatenate([jnp.cos(ang), jnp.cos(ang)], axis=-1)
    sin = jnp.concatenate([-jnp.sin(ang), jnp.sin(ang)], axis=-1)

    bw = hp * HEAD_DIM
    zspec = lambda c: pl.BlockSpec((seq, bw), lambda b, h: (rb0 + b, (c + h * hp) // hp))
    rope_spec = pl.BlockSpec((seq, HEAD_DIM), lambda b, h: (0, 0))
    smem = pl.BlockSpec(memory_space=pltpu.SMEM)
    if s0.ndim == 5:
        s0_spec = pl.BlockSpec((None, None, hp, HEAD_DIM, HEAD_DIM), lambda b, h: (layer, b, h, 0, 0))
    else:
        s0_spec = pl.BlockSpec((None, hp, HEAD_DIM, HEAD_DIM), lambda b, h: (b, h, 0, 0))
    gain_spec = pl.BlockSpec((None, 1, bw), lambda b, h: (layer, 0, h))
    in_specs = [smem, smem, zspec(col0), zspec(col0 + heads), zspec(col0 + 2 * heads), zspec(col0 + 3 * heads),
                rope_spec, rope_spec, gain_spec, s0_spec]
    args = [log_g, g_block, z, z, z, z, cos, sin, gain, s0]
    aliases = {}
    for out_index, arr in enumerate((prev, finals)):
        if arr is not None:
            in_specs.append(pl.BlockSpec(memory_space=pl.ANY))
            args.append(arr)
            aliases[len(args) - 1] = out_index
    return pl.pallas_call(
        functools.partial(_ret_kernel, cb=cb, nblk=nblk, hp=hp, n_aliased=len(aliases)),
        grid=(batch, heads // hp),
        in_specs=in_specs,
        out_specs=[pl.BlockSpec((seq, bw), lambda b, h: (rb0 + b, h)),
                   pl.BlockSpec((None, None, hp, HEAD_DIM, HEAD_DIM), lambda b, h: (layer, b, h, 0, 0))],
        out_shape=[jax.ShapeDtypeStruct((t_total, width), BF16),
                   jax.ShapeDtypeStruct((depth, batch, heads, HEAD_DIM, HEAD_DIM), F32)],
        scratch_shapes=[pltpu.VMEM((cb, cb), F32)] if nblk > 1 else [],
        input_output_aliases=aliases,
        compiler_params=_params("arbitrary", "arbitrary"),
        name="retention_prompt" if prev is None else "retention_sample",
    )(*args)


def _pool_kernel(u_ref, halo_ref, w_ref, sc_ref, *rest, pos0, nseq, first_tile_has_no_history, aliased):
    if aliased:
        rest = rest[1:]
    o_ref, x_ref = rest
    i, g = pl.program_id(1), pl.program_id(2)
    tr = u_ref.shape[0]
    sr = tr // nseq
    for s in range(nseq):
        halo = halo_ref[s] if nseq > 1 else halo_ref[...]
        if first_tile_has_no_history:
            halo = jnp.where(i == 0, 0.0, halo)
        x_ref[s, 0:POOL_HALO, :] = halo
        x_ref[s, POOL_HALO:POOL_HALO + sr, :] = u_ref[s * sr:(s + 1) * sr, :]
    pos = (pos0 + i * sr + lax.broadcasted_iota(jnp.int32, (sr, 1), 0)).astype(F32)

    for gi, w in enumerate(POOL_WINDOWS):
        @pl.when(g == gi)
        def _(w=w):
            inv_cnt = 1.0 / jnp.minimum(float(w), pos + 1.0)
            parts = []
            for s in range(nseq):
                win = x_ref[s, POOL_HALO:POOL_HALO + sr, :]
                for j in range(1, w):
                    win = win + x_ref[s, POOL_HALO - j:POOL_HALO - j + sr, :]
                parts.append(win * inv_cnt - x_ref[s, POOL_HALO:POOL_HALO + sr, :])
            diff = parts[0] if nseq == 1 else jnp.concatenate(parts, axis=0)
            y = jnp.dot(diff.astype(BF16), w_ref[...].astype(BF16), preferred_element_type=F32)
            o_ref[...] = (y * sc_ref[...]).astype(o_ref.dtype)


def _pool(z, halo, w_pool, scale, prev, layer, *, row0, batch, seq, tr, pos0, col0, t_total):
    groups, gw = w_pool.shape[1], w_pool.shape[2]
    assert groups == len(POOL_WINDOWS) and max(POOL_WINDOWS) - 1 <= POOL_HALO and tr % POOL_HALO == 0
    in_specs = []
    if halo is None:
        nseq, nt, nb = 1, seq // tr, batch
        rb0 = row0 // tr
        per = tr // POOL_HALO
        in_specs += [pl.BlockSpec((tr, gw), lambda b, i, g: (rb0 + b * nt + i, col0 + g)),
                     pl.BlockSpec((POOL_HALO, gw),
                                  lambda b, i, g: (jnp.maximum((rb0 + b * nt + i) * per - 1, 0), col0 + g))]
        args = [z, z]
    else:
        assert tr == batch * seq and row0 % tr == 0
        nseq, nt, nb = batch, 1, 1
        rb0 = row0 // tr
        in_specs += [pl.BlockSpec((tr, gw), lambda b, i, g: (rb0, col0 + g)),
                     pl.BlockSpec((None, batch, POOL_HALO, gw), lambda b, i, g: (layer, 0, 0, g))]
        args = [z, halo]
    in_specs += [pl.BlockSpec((None, None, gw, gw), lambda b, i, g: (layer, g, 0, 0)),
                 pl.BlockSpec((None, 1, gw), lambda b, i, g: (layer, 0, g))]
    args += [w_pool, scale]
    aliases = {}
    if prev is not None:
        in_specs.append(pl.BlockSpec(memory_space=pl.ANY))
        args.append(prev)
        aliases = {len(args) - 1: 0}
    return pl.pallas_call(
        functools.partial(_pool_kernel, pos0=pos0, nseq=nseq, first_tile_has_no_history=halo is None,
                          aliased=prev is not None),
        grid=(nb, nt, groups),
        in_specs=in_specs,
        out_specs=pl.BlockSpec((tr, gw), lambda b, i, g: (rb0 + b * nt + i, g)),
        out_shape=jax.ShapeDtypeStruct((t_total, groups * gw), BF16),
        scratch_shapes=[pltpu.VMEM((nseq, POOL_HALO + tr // nseq, gw), F32)],
        input_output_aliases=aliases,
        compiler_params=_params("arbitrary", "arbitrary", "arbitrary"),
        name="pool_prompt" if prev is None else "pool_sample",
    )(*args)


def kernel(x_prompt, x_sample, cache_band_k, cache_band_v, state_retention, state_pool, norm_mix, w_in,
           rel_bias_table, ret_norm, pool_w, pool_scale, w_out, norm_ffn, w_up, w_down, norm_final):
    bp, lp, d = x_prompt.shape
    bs, ls, _ = x_sample.shape
    depth = w_in.shape[0]
    a_heads = rel_bias_table.shape[1]
    a_width = a_heads * HEAD_DIM
    b_width = ret_norm.shape[1]
    b_heads = b_width // HEAD_DIM
    gw = pool_w.shape[2]
    tp, ts = bp * lp, bs * ls
    t_total = tp + ts
    keep = min(BAND, lp)
    pool_ctx = state_pool.shape[2]
    assert a_width % gw == 0 and (3 * a_width + 4 * b_width) % gw == 0 and tp % ts == 0
    bq_col = 3 * a_heads
    cu_col = (3 * a_width + 4 * b_width) // gw
    cu0 = cu_col * gw

    pos_p = jnp.arange(lp, dtype=F32)
    pos_s = PAST_LEN + jnp.arange(ls, dtype=F32)
    zero_state = jnp.zeros((bp, b_heads, HEAD_DIM, HEAD_DIM), F32)
    pool_halo = jnp.pad(state_pool, ((0, 0), (0, 0), (POOL_HALO - pool_ctx, 0), (0, 0)))
    cache_k = jnp.transpose(cache_band_k, (0, 1, 3, 2, 4))
    cache_v = jnp.transpose(cache_band_v, (0, 1, 3, 2, 4))
    norm_mix, norm_ffn, ret_norm, pool_scale = (
        g.reshape(depth, 1, -1) for g in (norm_mix, norm_ffn, ret_norm, pool_scale))

    def tail(z, b, rows, c0, c1):
        return lax.slice(z, ((b + 1) * lp - rows, c0), ((b + 1) * lp, c1))

    pp_l, ps_l = [], []
    tails = news = ret_p = ret_s = None
    for l in range(depth):
        if l == 0:
            h, xn = _stage([x_prompt.reshape(tp, d), x_sample.reshape(ts, d)], norm_mix, l)
        else:
            xn, = _stage([h], norm_mix, l)
        z = _wide_matmul(xn, w_in, l, relu2=False, out_dtype=F32)

        a_o, *tails = _attn_prompt(z, rel_bias_table, tails, l, depth=depth, batch=bp, seq=lp, keep=keep,
                                   heads=a_heads, t_total=t_total)
        a_o, *news = _attn_sample(z, cache_k, cache_v, rel_bias_table, a_o, news, l,
                                  row0=tp, batch=bs, s_len=ls, heads=a_heads)
        b_o, ret_p = _retention(z, zero_state, ret_norm, pos_p, None, ret_p, l, depth=depth, row0=0, batch=bp,
                                seq=lp, cb=RET_BLOCK, hp=1, heads=b_heads, col0=bq_col, t_total=t_total)
        b_o, ret_s = _retention(z, state_retention, ret_norm, pos_s, b_o, ret_s, l, depth=depth, row0=tp, batch=bs,
                                seq=ls, cb=ls, hp=b_heads, heads=b_heads, col0=bq_col, t_total=t_total)
        c_o = _pool(z, None, pool_w, pool_scale, None, l, row0=0, batch=bp, seq=lp,
                    tr=_pick_tile(lp, POOL_ROWS, POOL_HALO), pos0=0, col0=cu_col, t_total=t_total)
        c_o = _pool(z, pool_halo, pool_w, pool_scale, c_o, l, row0=tp, batch=bs, seq=ls,
                    tr=ts, pos0=PAST_LEN, col0=cu_col, t_total=t_total)

        h, hg, ssq = _outproj(h, a_o, b_o, c_o, w_out, norm_ffn, l)
        act = _wide_matmul(hg, w_up, l, relu2=True, out_dtype=BF16, panel=_pick_tile(w_up.shape[2], DOWN_K, LANES))
        h = _down_residual(h, ssq, act, w_down, l)

        pp_l.append(jnp.stack([tail(z, b, pool_ctx, cu0, z.shape[1]) for b in range(bp)]))
        cu_s = lax.slice(z, (tp, cu0), (t_total, z.shape[1])).reshape(bs, ls, -1)
        ps_l.append(jnp.concatenate([state_pool[l], cu_s], axis=1)[:, -pool_ctx:])

    y_prompt = _rmsnorm(h, norm_final, row0=0, rows=tp)
    y_sample = _rmsnorm(h, norm_final, row0=tp, rows=ts)
    rows_major = lambda x: jnp.transpose(x, (0, 1, 3, 2, 4))
    return (y_prompt.reshape(bp, lp, d), y_sample.reshape(bs, ls, d),
            rows_major(tails[0]), rows_major(tails[1]), ret_p, jnp.stack(pp_l),
            rows_major(news[0]), rows_major(news[1]), ret_s, jnp.stack(ps_l))
```

```python
import functools
import math

import jax
import jax.numpy as jnp
from jax import lax
from jax.experimental import pallas as pl
from jax.experimental.pallas import tpu as pltpu

F32 = jnp.float32
BF16 = jnp.bfloat16

PAST_LEN = 1024
CHUNK = 64
BAND_CHUNKS = 8
BAND = BAND_CHUNKS * CHUNK
HEAD_DIM = 128
REL_CLIP = 128
TABLE = 2 * REL_CLIP + 1
POOL_WINDOWS = (2, 4, 8, 16)
POOL_HALO = 16
ROPE_BASE = 10000.0
EPS = 1e-6
NEG_INF = -1e30
LOG2_E = 1.4426950408889634

LANES = 128
SUBLANES_BF16 = 16
MXU_COLS = 256
VMEM_LIMIT_BYTES = 60 * 1024 * 1024

ATTN_TQ = 256
RET_BLOCK = 256
ATTN_UNROLL = 8
RET_UNROLL = 4
DOWN_K = 1024
STAGE_ROWS = 256
NORM_ROWS = 64
ROW_TILE = 1100
WIDE_ROW_TILE = 2200
POOL_ROWS = 2048


def _pick_tile(n, target, align):
    best = None
    for t in range(align, min(n, target) + 1, align):
        if n % t == 0:
            best = t
    if best is None:
        raise ValueError(f"no tile for {n} (align {align}, target {target})")
    return best


def _params(*sem):
    return pltpu.CompilerParams(dimension_semantics=sem, vmem_limit_bytes=VMEM_LIMIT_BYTES)


def _fold_lanes(y):
    return functools.reduce(jnp.add, [y[:, c:c + LANES] for c in range(0, y.shape[1], LANES)])


def _stage_kernel(*refs, n_first, copy):
    n_out = 2 if copy else 1
    x_refs, g_ref, outs = refs[:-n_out - 1], refs[-n_out - 1], refs[-n_out:]

    def emit(x_ref):
        def body(c, carry):
            rows = pl.ds(pl.multiple_of(c * NORM_ROWS, NORM_ROWS), NORM_ROWS)
            x = x_ref[rows, :]
            if copy:
                outs[0][rows, :] = x
            ms = jnp.mean(x * x, axis=-1, keepdims=True)
            outs[-1][rows, :] = (x * lax.rsqrt(ms + EPS) * g_ref[...]).astype(BF16)
            return carry
        lax.fori_loop(0, x_ref.shape[0] // NORM_ROWS, body, 0)

    if len(x_refs) == 1:
        emit(x_refs[0])
    else:
        @pl.when(pl.program_id(0) < n_first)
        def _():
            emit(x_refs[0])

        @pl.when(pl.program_id(0) >= n_first)
        def _():
            emit(x_refs[1])


def _stage(xs, g, layer):
    d = xs[0].shape[1]
    tm = _pick_tile(math.gcd(*[x.shape[0] for x in xs]), STAGE_ROWS, NORM_ROWS)
    n_first = xs[0].shape[0] // tm
    t = sum(x.shape[0] for x in xs)
    copy = len(xs) > 1
    in_specs = [pl.BlockSpec((tm, d), lambda i: (jnp.minimum(i, n_first - 1), 0))]
    if copy:
        in_specs.append(pl.BlockSpec((tm, d), lambda i: (jnp.maximum(i - n_first, 0), 0)))
    in_specs.append(pl.BlockSpec((None, 1, d), lambda i: (layer, 0, 0)))
    out_specs = [pl.BlockSpec((tm, d), lambda i: (i, 0))]
    out_shape = [jax.ShapeDtypeStruct((t, d), BF16)]
    if copy:
        out_specs.insert(0, pl.BlockSpec((tm, d), lambda i: (i, 0)))
        out_shape.insert(0, jax.ShapeDtypeStruct((t, d), F32))
    return pl.pallas_call(
        functools.partial(_stage_kernel, n_first=n_first, copy=copy),
        grid=(t // tm,),
        in_specs=in_specs,
        out_specs=out_specs,
        out_shape=out_shape,
        compiler_params=_params("parallel"),
        name="stage_rows",
    )(*xs, g)


def _matmul_kernel(x_ref, w_ref, o_ref, *, relu2):
    for c in range(0, o_ref.shape[1], MXU_COLS):
        y = jnp.dot(x_ref[...], w_ref[:, c:c + MXU_COLS].astype(BF16), preferred_element_type=F32)
        if relu2:
            y = jnp.square(jnp.maximum(y, 0.0))
        o_ref[:, c:c + MXU_COLS] = y.astype(o_ref.dtype)


def _wide_matmul(x, w, layer, *, relu2, out_dtype):
    t, k = x.shape
    n = w.shape[2]
    tm = _pick_tile(t, WIDE_ROW_TILE, SUBLANES_BF16)
    tn = _pick_tile(n, 512, MXU_COLS)
    return pl.pallas_call(
        functools.partial(_matmul_kernel, relu2=relu2),
        grid=(t // tm, n // tn),
        in_specs=[pl.BlockSpec((tm, k), lambda i, j: (i, 0), pipeline_mode=pl.Buffered(1)),
                  pl.BlockSpec((None, k, tn), lambda i, j: (layer, 0, j))],
        out_specs=pl.BlockSpec((tm, tn), lambda i, j: (i, j)),
        out_shape=jax.ShapeDtypeStruct((t, n), out_dtype),
        compiler_params=_params("parallel", "parallel"),
        name="matmul_relu2" if relu2 else "matmul",
    )(x, w)


def _outproj_kernel(a_ref, b_ref, c_ref, wa_ref, wb_ref, wc_ref, h_ref, g_ref, o_ref, hg_ref, ssq_ref):
    ssq = None
    for c in range(0, o_ref.shape[1], MXU_COLS):
        cols = slice(c, c + MXU_COLS)
        acc = jnp.dot(a_ref[...], wa_ref[:, cols].astype(BF16), preferred_element_type=F32)
        acc += jnp.dot(b_ref[...], wb_ref[:, cols].astype(BF16), preferred_element_type=F32)
        acc += jnp.dot(c_ref[...], wc_ref[:, cols].astype(BF16), preferred_element_type=F32)
        hn = h_ref[:, cols] + acc
        o_ref[:, cols] = hn
        hg_ref[:, cols] = (hn * g_ref[:, cols]).astype(BF16)
        part = _fold_lanes(hn * hn)
        ssq = part if ssq is None else ssq + part

    @pl.when(pl.program_id(1) == 0)
    def _():
        ssq_ref[...] = ssq

    @pl.when(pl.program_id(1) > 0)
    def _():
        ssq_ref[...] += ssq


def _outproj(h, a_o, b_o, c_o, w_out, g, layer):
    t, d = h.shape
    wa, wb, wc = a_o.shape[1], b_o.shape[1], c_o.shape[1]
    assert wa == wb and (wa + wb) % wc == 0
    tm = _pick_tile(t, ROW_TILE, SUBLANES_BF16)
    tn = _pick_tile(d, 512, MXU_COLS)
    return pl.pallas_call(
        _outproj_kernel,
        grid=(t // tm, d // tn),
        in_specs=[
            pl.BlockSpec((tm, wa), lambda i, j: (i, 0)),
            pl.BlockSpec((tm, wb), lambda i, j: (i, 0)),
            pl.BlockSpec((tm, wc), lambda i, j: (i, 0)),
            pl.BlockSpec((None, wa, tn), lambda i, j: (layer, 0, j)),
            pl.BlockSpec((None, wb, tn), lambda i, j: (layer, 1, j)),
            pl.BlockSpec((None, wc, tn), lambda i, j: (layer, (wa + wb) // wc, j)),
            pl.BlockSpec((tm, tn), lambda i, j: (i, j)),
            pl.BlockSpec((None, 1, tn), lambda i, j: (layer, 0, j)),
        ],
        out_specs=[pl.BlockSpec((tm, tn), lambda i, j: (i, j)),
                   pl.BlockSpec((tm, tn), lambda i, j: (i, j)),
                   pl.BlockSpec((tm, LANES), lambda i, j: (i, 0))],
        out_shape=[jax.ShapeDtypeStruct((t, d), F32), jax.ShapeDtypeStruct((t, d), BF16),
                   jax.ShapeDtypeStruct((t, LANES), F32)],
        compiler_params=_params("parallel", "arbitrary"),
        name="outproj_residual",
    )(a_o, b_o, c_o, w_out, w_out, w_out, h, g)


def _down_kernel(a_ref, w_ref, h_ref, ssq_ref, o_ref, *, nk, norm_dim):
    k = pl.program_id(2)
    col_chunks = [slice(c, c + MXU_COLS) for c in range(0, o_ref.shape[1], MXU_COLS)]

    def partial_product(cols):
        return jnp.dot(a_ref[...], w_ref[:, cols].astype(BF16), preferred_element_type=F32)

    @pl.when(k == 0)
    def _():
        for cols in col_chunks:
            o_ref[:, cols] = partial_product(cols)

    @pl.when((k > 0) & (k < nk - 1))
    def _():
        for cols in col_chunks:
            o_ref[:, cols] += partial_product(cols)

    @pl.when(k == nk - 1)
    def _():
        ms = jnp.sum(ssq_ref[...], axis=-1, keepdims=True) * (1.0 / norm_dim)
        r2 = 1.0 / (ms + EPS)
        for cols in col_chunks:
            o_ref[:, cols] = h_ref[:, cols] + (o_ref[:, cols] + partial_product(cols)) * r2


def _down_residual(h, ssq, a, w, layer):
    t, d = h.shape
    f = a.shape[1]
    tm = _pick_tile(t, WIDE_ROW_TILE, SUBLANES_BF16)
    tn = _pick_tile(d, 1024, MXU_COLS)
    tk = _pick_tile(f, DOWN_K, LANES)
    nk = f // tk
    assert nk >= 2
    return pl.pallas_call(
        functools.partial(_down_kernel, nk=nk, norm_dim=d),
        grid=(t // tm, d // tn, nk),
        in_specs=[
            pl.BlockSpec((tm, tk), lambda i, j, k: (i, k)),
            pl.BlockSpec((None, tk, tn), lambda i, j, k: (layer, k, j)),
            pl.BlockSpec((tm, tn), lambda i, j, k: (i, j)),
            pl.BlockSpec((tm, LANES), lambda i, j, k: (i, 0), pipeline_mode=pl.Buffered(1)),
        ],
        out_specs=pl.BlockSpec((tm, tn), lambda i, j, k: (i, j)),
        out_shape=jax.ShapeDtypeStruct((t, d), F32),
        compiler_params=_params("parallel", "parallel", "arbitrary"),
        name="down_residual",
    )(a, w, h, ssq)


def _rmsnorm_kernel(x_ref, g_ref, o_ref):
    x = x_ref[...]
    ms = jnp.mean(x * x, axis=-1, keepdims=True)
    o_ref[...] = x * lax.rsqrt(ms + EPS) * g_ref[...]


def _rmsnorm(x, g, *, row0, rows):
    d = x.shape[1]
    tm = _pick_tile(rows, 256, 8)
    assert row0 % tm == 0
    return pl.pallas_call(
        _rmsnorm_kernel,
        grid=(rows // tm,),
        in_specs=[pl.BlockSpec((tm, d), lambda i: (row0 // tm + i, 0)), pl.BlockSpec((1, d), lambda i: (0, 0))],
        out_specs=pl.BlockSpec((tm, d), lambda i: (i, 0)),
        out_shape=jax.ShapeDtypeStruct((rows, d), F32),
        compiler_params=_params("parallel"),
        name="final_rmsnorm",
    )(x, g.reshape(1, d))


def _toeplitz_bias(tbl_ref, base, rows, width):
    m = lax.broadcasted_iota(jnp.int32, (8, width), 1)
    idx = jnp.where(m >= BAND + rows, 2 * REL_CLIP, jnp.clip(BAND + REL_CLIP - m, 0, 2 * REL_CLIP))

    def body(d, g):
        return jnp.where(idx == d, tbl_ref[base + d], g)

    g = lax.fori_loop(0, TABLE, body, jnp.zeros((8, width), F32))
    full = jnp.broadcast_to(g[0:1, :], (rows, width))
    return pltpu.roll(full, 0, 1, stride=1, stride_axis=0)


def _nt_dot(a, b):
    return lax.dot_general(a, b, (((1,), (1,)), ((), ())), preferred_element_type=F32)


def _attn_prompt_kernel(tbl_ref, q_ref, k_ref, v_ref, *rest, layer, heads, tq, nkb):
    o_ref, kt_ref, vt_ref, bias_ref, kb_ref, vb_ref = rest[3:]
    head, b = pl.program_id(0), pl.program_id(1)
    seq = q_ref.shape[0]
    tk = nkb * tq
    keep = kt_ref.shape[0]
    kt_ref[...] = k_ref[seq - keep:seq, :]
    vt_ref[...] = v_ref[seq - keep:seq, :]

    @pl.when(b == 0)
    def _():
        bias = _toeplitz_bias(tbl_ref, (layer * heads + head) * TABLE, tq, tk + tq)[:, :tk]
        qc = lax.broadcasted_iota(jnp.int32, (tq, tk), 0) // CHUNK
        kc = lax.broadcasted_iota(jnp.int32, (tq, tk), 1) // CHUNK
        bias_ref[...] = jnp.where((kc >= qc) & (kc <= qc + BAND_CHUNKS), bias * LOG2_E, NEG_INF)

    kb_ref[...] = k_ref[...].astype(BF16)
    vb_ref[:, 0:HEAD_DIM] = v_ref[...].astype(BF16)
    vb_ref[:, HEAD_DIM:2 * HEAD_DIM] = jnp.ones((seq, HEAD_DIM), BF16)
    scale = HEAD_DIM ** -0.5 * LOG2_E

    def tile(q0, k0, nk, bias):
        q = (q_ref[pl.ds(q0, tq), :] * scale).astype(BF16)
        s = _nt_dot(q, kb_ref[pl.ds(k0, nk), :]) + bias
        m = jnp.max(s, axis=-1, keepdims=True)
        p = jnp.exp2(s - m)
        ol = jnp.dot(p.astype(BF16), vb_ref[pl.ds(k0, nk), :], preferred_element_type=F32)
        o_ref[pl.ds(q0, tq), :] = (ol[:, 0:HEAD_DIM] / ol[:, HEAD_DIM:2 * HEAD_DIM]).astype(o_ref.dtype)

    for i in range(nkb - 1):
        nk = (i + 1) * tq
        tile(i * tq, 0, nk, bias_ref[:, tk - nk:tk])

    def body(i, carry):
        q0 = pl.multiple_of(i * tq, tq)
        k0 = pl.multiple_of((i - (nkb - 1)) * tq, tq)
        tile(q0, k0, tk, bias_ref[...])
        return carry

    n_loop = seq // tq - (nkb - 1)
    unroll = max(u for u in range(1, ATTN_UNROLL + 1) if n_loop % u == 0)
    lax.fori_loop(nkb - 1, seq // tq, body, 0, unroll=unroll)


def _chain(prev_arrays, first_output):
    if prev_arrays is None:
        return [], [], {}
    n = len(prev_arrays)
    return [pl.BlockSpec(memory_space=pl.ANY)] * n, list(prev_arrays), {i: first_output + i for i in range(n)}


def _attn_prompt(z, table, prev, tails, layer, *, depth, batch, seq, keep, heads, t_total):
    tq = ATTN_TQ
    assert seq % tq == 0 and BAND % tq == 0 and tq % CHUNK == 0 and seq // tq >= BAND // tq + 1
    nkb = BAND // tq + 1
    width = heads * HEAD_DIM
    spec = lambda c: pl.BlockSpec((seq, HEAD_DIM), lambda h, b: (b, c + h))
    tail_spec = pl.BlockSpec((None, None, None, keep, HEAD_DIM), lambda h, b: (layer, b, h, 0, 0))
    tail_shape = jax.ShapeDtypeStruct((depth, batch, heads, keep, HEAD_DIM), F32)
    chain_specs, chain_args, chain_aliases = _chain([prev, *tails], 0)
    in_specs = [pl.BlockSpec(memory_space=pltpu.SMEM), spec(0), spec(heads), spec(2 * heads)]
    return pl.pallas_call(
        functools.partial(_attn_prompt_kernel, layer=layer, heads=heads, tq=tq, nkb=nkb),
        grid=(heads, batch),
        in_specs=in_specs + chain_specs,
        out_specs=[pl.BlockSpec((seq, HEAD_DIM), lambda h, b: (b, h)), tail_spec, tail_spec],
        out_shape=[jax.ShapeDtypeStruct((t_total, width), BF16), tail_shape, tail_shape],
        scratch_shapes=[pltpu.VMEM((tq, nkb * tq), F32), pltpu.VMEM((seq, HEAD_DIM), BF16),
                        pltpu.VMEM((seq, 2 * HEAD_DIM), BF16)],
        input_output_aliases={len(in_specs) + i: o for i, o in chain_aliases.items()},
        compiler_params=_params("arbitrary", "arbitrary"),
        name="band_attention_prompt",
    )(table.reshape(-1), z, z, z, *chain_args)


def _attn_sample_kernel(tbl_ref, q_ref, kn_ref, vn_ref, kc_ref, vc_ref, *rest, layer, heads):
    o_ref, ks_ref, vs_ref, bias_ref = rest[3:]
    s_len = q_ref.shape[0]
    la = kc_ref.shape[1]
    for h in range(heads):
        ks_ref[h] = kn_ref[:, h * HEAD_DIM:(h + 1) * HEAD_DIM]
        vs_ref[h] = vn_ref[:, h * HEAD_DIM:(h + 1) * HEAD_DIM]

    @pl.when(pl.program_id(0) == 0)
    def _():
        for h in range(heads):
            bias_ref[h] = _toeplitz_bias(tbl_ref, (layer * heads + h) * TABLE, s_len, bias_ref.shape[2])

    scale = HEAD_DIM ** -0.5
    hs = range(heads)
    sl = [slice(h * HEAD_DIM, (h + 1) * HEAD_DIM) for h in hs]
    q = [q_ref[:, sl[h]].astype(BF16) for h in hs]
    sc = [_nt_dot(q[h], kc_ref[h].astype(BF16)) * scale + bias_ref[h, :, 0:la] for h in hs]
    sn = [_nt_dot(q[h], kn_ref[:, sl[h]].astype(BF16)) * scale + bias_ref[h, :, la:la + s_len] for h in hs]
    m = [jnp.maximum(jnp.max(sc[h], axis=-1, keepdims=True), jnp.max(sn[h], axis=-1, keepdims=True)) for h in hs]
    pc = [jnp.exp(sc[h] - m[h]) for h in hs]
    pn = [jnp.exp(sn[h] - m[h]) for h in hs]
    l = [jnp.sum(pc[h], axis=-1, keepdims=True) + jnp.sum(pn[h], axis=-1, keepdims=True) for h in hs]
    o = [jnp.dot(pc[h].astype(BF16), vc_ref[h].astype(BF16), preferred_element_type=F32)
         + jnp.dot(pn[h].astype(BF16), vn_ref[:, sl[h]].astype(BF16), preferred_element_type=F32) for h in hs]
    for h in hs:
        o_ref[:, sl[h]] = (o[h] / l[h]).astype(o_ref.dtype)


def _attn_sample(z, cache_k, cache_v, table, prev, news, layer, *, row0, batch, s_len, heads):
    depth, la = cache_k.shape[0], cache_k.shape[3]
    assert la == BAND, "the relative-position bias tile assumes a full band of cached rows"
    width = heads * HEAD_DIM
    rb0 = row0 // s_len
    bias_w = pl.cdiv(la + 2 * s_len, LANES) * LANES
    new_spec = lambda c: pl.BlockSpec((s_len, width), lambda b: (rb0 + b, c))
    cache_spec = pl.BlockSpec((None, None, heads, la, HEAD_DIM), lambda b: (layer, b, 0, 0, 0))
    rows_spec = pl.BlockSpec((None, None, heads, s_len, HEAD_DIM), lambda b: (layer, b, 0, 0, 0))
    rows_shape = jax.ShapeDtypeStruct((depth, batch, heads, s_len, HEAD_DIM), F32)
    chain_specs, chain_args, chain_aliases = _chain(news, 1)
    in_specs = [pl.BlockSpec(memory_space=pltpu.SMEM), new_spec(0), new_spec(1), new_spec(2),
                cache_spec, cache_spec, pl.BlockSpec(memory_space=pl.ANY)]
    aliases = {len(in_specs) - 1: 0}
    aliases.update({len(in_specs) + i: o for i, o in chain_aliases.items()})
    return pl.pallas_call(
        functools.partial(_attn_sample_kernel, layer=layer, heads=heads),
        grid=(batch,),
        in_specs=in_specs + chain_specs,
        out_specs=[pl.BlockSpec((s_len, width), lambda b: (rb0 + b, 0)), rows_spec, rows_spec],
        out_shape=[jax.ShapeDtypeStruct(prev.shape, prev.dtype), rows_shape, rows_shape],
        scratch_shapes=[pltpu.VMEM((heads, s_len, bias_w), F32)],
        input_output_aliases=aliases,
        compiler_params=_params("arbitrary"),
        name="band_attention_sample",
    )(table.reshape(-1), z, z, z, cache_k, cache_v, prev, *chain_args)


def _ret_kernel(lg_ref, gb_ref, q_ref, k_ref, v_ref, gate_ref, cos_ref, sin_ref, gain_ref, s0_ref, *rest,
                cb, nblk, hp, n_aliased):
    rest = rest[n_aliased:]
    o_ref, sfin_ref = rest[0], rest[1]
    hgroup = pl.program_id(1)
    row = lax.broadcasted_iota(jnp.int32, (cb, cb), 0)
    col = lax.broadcasted_iota(jnp.int32, (cb, cb), 1)
    diff = (row - col).astype(F32)
    t = lax.broadcasted_iota(jnp.int32, (cb, HEAD_DIM), 0).astype(F32)

    for hh in range(hp):
        head = hgroup * hp + hh
        log_g = lg_ref[head]
        g_block = gb_ref[head]
        sl = slice(hh * HEAD_DIM, (hh + 1) * HEAD_DIM)
        key_scale = HEAD_DIM ** -0.5
        decay = jnp.where(diff >= 0, jnp.exp(jnp.maximum(diff, 0.0) * log_g) * key_scale, 0.0)
        if nblk > 1:
            decay_ref = rest[2]
            decay_ref[...] = decay
        q_w = jnp.exp((t + 1.0) * log_g)
        k_w = jnp.exp((cb - 1.0 - t) * log_g) * key_scale
        gain = gain_ref[:, sl]

        def block(n, state, sl=sl, q_w=q_w, k_w=k_w, gain=gain, decay=decay):
            r = pl.multiple_of(n * cb, cb)
            rows = pl.ds(r, cb)
            cos, sin = cos_ref[rows, :], sin_ref[rows, :]

            def rot(x):
                return x * cos + pltpu.roll(x, HEAD_DIM // 2, 1) * sin

            q = rot(q_ref[rows, sl])
            k = rot(k_ref[rows, sl])
            v = v_ref[rows, sl].astype(BF16)
            qb = q.astype(BF16)
            dec = rest[2][...] if nblk > 1 else decay
            scores = _nt_dot(qb, k.astype(BF16)) * dec
            o = jnp.dot(scores.astype(BF16), v, preferred_element_type=F32)
            o += jnp.dot(qb, state.astype(BF16), preferred_element_type=F32) * q_w
            kv = lax.dot_general((k * k_w).astype(BF16), v, (((0,), (0,)), ((), ())), preferred_element_type=F32)
            mu = jnp.mean(o, axis=-1, keepdims=True)
            var = jnp.mean(jnp.square(o - mu), axis=-1, keepdims=True)
            on = (o - mu) * lax.rsqrt(var + EPS) * gain
            gate = gate_ref[rows, sl]
            o_ref[rows, sl] = (gate * jax.nn.sigmoid(gate) * on).astype(o_ref.dtype)
            return state * g_block + kv

        state = s0_ref[hh]
        if nblk > 1:
            state = lax.fori_loop(0, nblk, block, state, unroll=RET_UNROLL)
        else:
            state = block(0, state)
        sfin_ref[hh] = state


def _retention(z, s0, gain, pos, prev, finals, layer, *, depth, row0, batch, seq, cb, hp, heads, col0, t_total):
    nblk = seq // cb
    assert seq % cb == 0 and heads % hp == 0 and row0 % seq == 0 and col0 % hp == 0
    rb0 = row0 // seq
    width = heads * HEAD_DIM
    hidx = jnp.arange(heads, dtype=F32)
    log_g = jnp.log1p(-jnp.exp2(-5.0 - hidx))
    g_block = jnp.exp(cb * log_g)
    half = HEAD_DIM // 2
    inv = ROPE_BASE ** (-jnp.arange(half, dtype=F32) / half)
    ang = pos[:, None] * inv[None, :]
    cos = jnp.concatenate([jnp.cos(ang), jnp.cos(ang)], axis=-1)
    sin = jnp.concatenate([-jnp.sin(ang), jnp.sin(ang)], axis=-1)

    bw = hp * HEAD_DIM
    zspec = lambda c: pl.BlockSpec((seq, bw), lambda b, h: (rb0 + b, (c + h * hp) // hp))
    rope_spec = pl.BlockSpec((seq, HEAD_DIM), lambda b, h: (0, 0))
    smem = pl.BlockSpec(memory_space=pltpu.SMEM)
    if s0.ndim == 5:
        s0_spec = pl.BlockSpec((None, None, hp, HEAD_DIM, HEAD_DIM), lambda b, h: (layer, b, h, 0, 0))
    else:
        s0_spec = pl.BlockSpec((None, hp, HEAD_DIM, HEAD_DIM), lambda b, h: (b, h, 0, 0))
    gain_spec = pl.BlockSpec((None, 1, bw), lambda b, h: (layer, 0, h))
    in_specs = [smem, smem, zspec(col0), zspec(col0 + heads), zspec(col0 + 2 * heads), zspec(col0 + 3 * heads),
                rope_spec, rope_spec, gain_spec, s0_spec]
    args = [log_g, g_block, z, z, z, z, cos, sin, gain, s0]
    aliases = {}
    for out_index, arr in enumerate((prev, finals)):
        if arr is not None:
            in_specs.append(pl.BlockSpec(memory_space=pl.ANY))
            args.append(arr)
            aliases[len(args) - 1] = out_index
    return pl.pallas_call(
        functools.partial(_ret_kernel, cb=cb, nblk=nblk, hp=hp, n_aliased=len(aliases)),
        grid=(batch, heads // hp),
        in_specs=in_specs,
        out_specs=[pl.BlockSpec((seq, bw), lambda b, h: (rb0 + b, h)),
                   pl.BlockSpec((None, None, hp, HEAD_DIM, HEAD_DIM), lambda b, h: (layer, b, h, 0, 0))],
        out_shape=[jax.ShapeDtypeStruct((t_total, width), BF16),
                   jax.ShapeDtypeStruct((depth, batch, heads, HEAD_DIM, HEAD_DIM), F32)],
        scratch_shapes=[pltpu.VMEM((cb, cb), F32)] if nblk > 1 else [],
        input_output_aliases=aliases,
        compiler_params=_params("arbitrary", "arbitrary"),
        name="retention_prompt" if nblk > 1 else "retention_sample",
    )(*args)


def _pool_kernel(u_ref, halo_ref, w_ref, sc_ref, *rest, pos0, nseq, first_tile_has_no_history, aliased):
    if aliased:
        rest = rest[1:]
    o_ref, x_ref = rest
    i, g = pl.program_id(1), pl.program_id(2)
    tr = u_ref.shape[0]
    sr = tr // nseq
    for s in range(nseq):
        halo = halo_ref[s] if nseq > 1 else halo_ref[...]
        if first_tile_has_no_history:
            halo = jnp.where(i == 0, 0.0, halo)
        x_ref[s, 0:POOL_HALO, :] = halo
        x_ref[s, POOL_HALO:POOL_HALO + sr, :] = u_ref[s * sr:(s + 1) * sr, :]
    pos = (pos0 + i * sr + lax.broadcasted_iota(jnp.int32, (sr, 1), 0)).astype(F32)

    for gi, w in enumerate(POOL_WINDOWS):
        @pl.when(g == gi)
        def _(w=w):
            inv_cnt = 1.0 / jnp.minimum(float(w), pos + 1.0)
            parts = []
            for s in range(nseq):
                win = x_ref[s, POOL_HALO:POOL_HALO + sr, :]
                for j in range(1, w):
                    win = win + x_ref[s, POOL_HALO - j:POOL_HALO - j + sr, :]
                parts.append(win * inv_cnt - x_ref[s, POOL_HALO:POOL_HALO + sr, :])
            diff = parts[0] if nseq == 1 else jnp.concatenate(parts, axis=0)
            y = jnp.dot(diff.astype(BF16), w_ref[...].astype(BF16), preferred_element_type=F32)
            o_ref[...] = (y * sc_ref[...]).astype(o_ref.dtype)


def _pool(z, halo, w_pool, scale, prev, layer, *, row0, batch, seq, tr, pos0, col0, t_total):
    groups, gw = w_pool.shape[1], w_pool.shape[2]
    assert groups == len(POOL_WINDOWS) and max(POOL_WINDOWS) - 1 <= POOL_HALO and tr % POOL_HALO == 0
    in_specs = []
    if halo is None:
        nseq, nt, nb = 1, seq // tr, batch
        rb0 = row0 // tr
        per = tr // POOL_HALO
        in_specs += [pl.BlockSpec((tr, gw), lambda b, i, g: (rb0 + b * nt + i, col0 + g)),
                     pl.BlockSpec((POOL_HALO, gw),
                                  lambda b, i, g: (jnp.maximum((rb0 + b * nt + i) * per - 1, 0), col0 + g))]
        args = [z, z]
    else:
        assert tr == batch * seq and row0 % tr == 0
        nseq, nt, nb = batch, 1, 1
        rb0 = row0 // tr
        in_specs += [pl.BlockSpec((tr, gw), lambda b, i, g: (rb0, col0 + g)),
                     pl.BlockSpec((None, batch, POOL_HALO, gw), lambda b, i, g: (layer, 0, 0, g))]
        args = [z, halo]
    in_specs += [pl.BlockSpec((None, None, gw, gw), lambda b, i, g: (layer, g, 0, 0)),
                 pl.BlockSpec((None, 1, gw), lambda b, i, g: (layer, 0, g))]
    args += [w_pool, scale]
    aliases = {}
    if prev is not None:
        in_specs.append(pl.BlockSpec(memory_space=pl.ANY))
        args.append(prev)
        aliases = {len(args) - 1: 0}
    return pl.pallas_call(
        functools.partial(_pool_kernel, pos0=pos0, nseq=nseq, first_tile_has_no_history=halo is None,
                          aliased=prev is not None),
        grid=(nb, nt, groups),
        in_specs=in_specs,
        out_specs=pl.BlockSpec((tr, gw), lambda b, i, g: (rb0 + b * nt + i, g)),
        out_shape=jax.ShapeDtypeStruct((t_total, groups * gw), BF16),
        scratch_shapes=[pltpu.VMEM((nseq, POOL_HALO + tr // nseq, gw), F32)],
        input_output_aliases=aliases,
        compiler_params=_params("arbitrary", "arbitrary", "arbitrary"),
        name="pool_prompt" if halo is None else "pool_sample",
    )(*args)


def kernel(x_prompt, x_sample, cache_band_k, cache_band_v, state_retention, state_pool, norm_mix, w_in,
           rel_bias_table, ret_norm, pool_w, pool_scale, w_out, norm_ffn, w_up, w_down, norm_final):
    bp, lp, d = x_prompt.shape
    bs, ls, _ = x_sample.shape
    depth = w_in.shape[0]
    a_heads = rel_bias_table.shape[1]
    a_width = a_heads * HEAD_DIM
    b_width = ret_norm.shape[1]
    b_heads = b_width // HEAD_DIM
    gw = pool_w.shape[2]
    tp, ts = bp * lp, bs * ls
    t_total = tp + ts
    keep = min(BAND, lp)
    pool_ctx = state_pool.shape[2]
    assert a_width % gw == 0 and (3 * a_width + 4 * b_width) % gw == 0 and tp % ts == 0
    bq_col = 3 * a_heads
    cu_col = (3 * a_width + 4 * b_width) // gw
    cu0 = cu_col * gw

    pos_p = jnp.arange(lp, dtype=F32)
    pos_s = PAST_LEN + jnp.arange(ls, dtype=F32)
    zero_state = jnp.zeros((bp, b_heads, HEAD_DIM, HEAD_DIM), F32)
    pool_halo = jnp.pad(state_pool, ((0, 0), (0, 0), (POOL_HALO - pool_ctx, 0), (0, 0)))
    cache_k = jnp.transpose(cache_band_k, (0, 1, 3, 2, 4))
    cache_v = jnp.transpose(cache_band_v, (0, 1, 3, 2, 4))
    norm_mix, norm_ffn, ret_norm, pool_scale = (
        g.reshape(depth, 1, -1) for g in (norm_mix, norm_ffn, ret_norm, pool_scale))

    def tail(z, b, rows, c0, c1):
        return lax.slice(z, ((b + 1) * lp - rows, c0), ((b + 1) * lp, c1))

    pp_l, ps_l = [], []
    a_o = jnp.zeros((t_total, a_width), BF16)
    b_o = jnp.zeros((t_total, b_width), BF16)
    c_o = jnp.zeros((t_total, gw * pool_w.shape[1]), BF16)
    tails = [jnp.zeros((depth, bp, a_heads, keep, HEAD_DIM), F32) for _ in range(2)]
    news = [jnp.zeros((depth, bs, a_heads, ls, HEAD_DIM), F32) for _ in range(2)]
    ret_p = jnp.zeros((depth, bp, b_heads, HEAD_DIM, HEAD_DIM), F32)
    ret_s = jnp.zeros((depth, bs, b_heads, HEAD_DIM, HEAD_DIM), F32)
    for l in range(depth):
        if l == 0:
            h, xn = _stage([x_prompt.reshape(tp, d), x_sample.reshape(ts, d)], norm_mix, l)
        else:
            xn, = _stage([h], norm_mix, l)
        z = _wide_matmul(xn, w_in, l, relu2=False, out_dtype=F32)

        a_o, *tails = _attn_prompt(z, rel_bias_table, a_o, tails, l, depth=depth, batch=bp, seq=lp, keep=keep,
                                   heads=a_heads, t_total=t_total)
        a_o, *news = _attn_sample(z, cache_k, cache_v, rel_bias_table, a_o, news, l,
                                  row0=tp, batch=bs, s_len=ls, heads=a_heads)
        b_o, ret_p = _retention(z, zero_state, ret_norm, pos_p, b_o, ret_p, l, depth=depth, row0=0, batch=bp,
                                seq=lp, cb=RET_BLOCK, hp=1, heads=b_heads, col0=bq_col, t_total=t_total)
        b_o, ret_s = _retention(z, state_retention, ret_norm, pos_s, b_o, ret_s, l, depth=depth, row0=tp, batch=bs,
                                seq=ls, cb=ls, hp=b_heads, heads=b_heads, col0=bq_col, t_total=t_total)
        c_o = _pool(z, None, pool_w, pool_scale, c_o, l, row0=0, batch=bp, seq=lp,
                    tr=_pick_tile(lp, POOL_ROWS, POOL_HALO), pos0=0, col0=cu_col, t_total=t_total)
        c_o = _pool(z, pool_halo, pool_w, pool_scale, c_o, l, row0=tp, batch=bs, seq=ls,
                    tr=ts, pos0=PAST_LEN, col0=cu_col, t_total=t_total)

        h, hg, ssq = _outproj(h, a_o, b_o, c_o, w_out, norm_ffn, l)
        act = _wide_matmul(hg, w_up, l, relu2=True, out_dtype=BF16)
        h = _down_residual(h, ssq, act, w_down, l)

        pp_l.append(jnp.stack([tail(z, b, pool_ctx, cu0, z.shape[1]) for b in range(bp)]))
        cu_s = lax.slice(z, (tp, cu0), (t_total, z.shape[1])).reshape(bs, ls, -1)
        ps_l.append(jnp.concatenate([state_pool[l], cu_s], axis=1)[:, -pool_ctx:])

    y_prompt = _rmsnorm(h, norm_final, row0=0, rows=tp)
    y_sample = _rmsnorm(h, norm_final, row0=tp, rows=ts)
    rows_major = lambda x: jnp.transpose(x, (0, 1, 3, 2, 4))
    return (y_prompt.reshape(bp, lp, d), y_sample.reshape(bs, ls, d),
            rows_major(tails[0]), rows_major(tails[1]), ret_p, jnp.stack(pp_l),
            rows_major(news[0]), rows_major(news[1]), ret_s, jnp.stack(ps_l))
```
